```python
import math
import jax, jax.numpy as jnp
from jax import lax
import numpy as np

D_MODEL = 1024
BATCH = 16
SEQ = 2048
DEPTH = 2
DEC_BATCH = 32
DEC_SEQ = 2048
PAST_LEN = 128

GRID_W = 64
N_HEADS = 8
N_KV_HEADS = 2
HEAD_DIM = 64
ATTN_W = N_HEADS * HEAD_DIM
KV_W = N_KV_HEADS * HEAD_DIM
ROPE_THETA = 10000.0
Q_BLOCK = 128
POOL_WINDOWS = (2, 4, 8, 16)
POOL_GROUP = 64
POOL_W = POOL_GROUP * len(POOL_WINDOWS)
CONV_W = 256
CONV_K = 31
N_BRANCH = 3
PROJ_W = ATTN_W + 2 * KV_W + POOL_W + 2 * CONV_W + N_BRANCH * D_MODEL
D_FF = 2816
N_EXPERTS = 8
TOP_K = 2
D_FF_EXPERT = 1408
N_DENSE = (DEPTH + 1) // 2
N_MOE = DEPTH // 2
ALPHA = (2 * DEPTH) ** 0.25
BETA = (8 * DEPTH) ** -0.25
LN_EPS = 1e-5
RMS_EPS = 1e-6

kernel_name = "hybrid_gated_encoder_two_batches"


def layer_norm(x, g, b):
    xf = x.astype(jnp.float32)
    mu = jnp.mean(xf, axis=-1, keepdims=True)
    var = jnp.mean(jnp.square(xf - mu), axis=-1, keepdims=True)
    y = (xf - mu) * lax.rsqrt(var + LN_EPS) * g.astype(jnp.float32) + b.astype(jnp.float32)
    return y.astype(x.dtype)


def rms_norm_f32(x, g):
    xf = x.astype(jnp.float32)
    return xf * lax.rsqrt(jnp.mean(xf * xf, axis=-1, keepdims=True) + RMS_EPS) * g.astype(jnp.float32)


def axial_rope_angles(rows):
    row = jnp.repeat(jnp.arange(rows), GRID_W).astype(jnp.float32)
    col = jnp.tile(jnp.arange(GRID_W), rows).astype(jnp.float32)
    n_freq = HEAD_DIM // 4
    inv = ROPE_THETA ** (-jnp.arange(n_freq, dtype=jnp.float32) / n_freq)
    return row[:, None] * inv, col[:, None] * inv


def rope_rotate(x, ang):
    n = ang.shape[-1]
    c = jnp.cos(ang)[:, None, :]
    s = jnp.sin(ang)[:, None, :]
    x1, x2 = x[..., :n], x[..., n:]
    return jnp.concatenate([x1 * c - x2 * s, x2 * c + x1 * s], axis=-1)


def apply_axial_rope(x, ang_row, ang_col):
    half = HEAD_DIM // 2
    return jnp.concatenate([rope_rotate(x[..., :half], ang_row), rope_rotate(x[..., half:], ang_col)], axis=-1)


def gqa_attention(q, k, v):
    b, s, _, _ = q.shape
    g = N_HEADS // N_KV_HEADS
    nb = s // Q_BLOCK
    qb = q.reshape(b, nb, Q_BLOCK, N_KV_HEADS, g, HEAD_DIM).transpose(1, 0, 2, 3, 4, 5)
    scale = HEAD_DIM ** -0.5

    def block(q_blk):
        sc = jnp.einsum('bqkgd,bskd->bkgqs', q_blk, k, preferred_element_type=jnp.float32) * scale
        p = jax.nn.softmax(sc, axis=-1)
        return jnp.einsum('bkgqs,bskd->bqkgd', p.astype(v.dtype), v)

    o = lax.map(block, qb)
    return o.transpose(1, 0, 2, 3, 4, 5).reshape(b, s, ATTN_W)


def multiscale_pool(u):
    b, s, _ = u.shape
    ng = len(POOL_WINDOWS)
    uf = u.astype(jnp.float32).reshape(b, s, ng, POOL_GROUP)
    cs = jnp.concatenate([jnp.zeros((b, 1, ng, POOL_GROUP), jnp.float32), jnp.cumsum(uf, axis=1)], axis=1)
    t = jnp.arange(s)
    outs = []
    for gi, w in enumerate(POOL_WINDOWS):
        lo = w // 2
        hi = w - lo - 1
        start = jnp.clip(t - lo, 0, s)
        end = jnp.clip(t + hi + 1, 0, s)
        csg = cs[:, :, gi]
        tot = jnp.take(csg, end, axis=1) - jnp.take(csg, start, axis=1)
        cnt = (end - start).astype(jnp.float32)[None, :, None]
        outs.append(tot / cnt - uf[:, :, gi])
    return jnp.stack(outs, axis=2)


def conformer_conv(a, gate, w_dw, b_dw, g_ln, b_ln):
    h = a * jax.nn.sigmoid(gate)
    h = lax.conv_general_dilated(h, w_dw[:, None, :].astype(h.dtype), window_strides=(1,),
                                 padding=[(CONV_K // 2, CONV_K // 2)],
                                 dimension_numbers=('NWC', 'WIO', 'NWC'),
                                 feature_group_count=CONV_W) + b_dw
    h = layer_norm(h, g_ln, b_ln)
    return jax.nn.silu(h)


def token_mixer(x, ang_row, ang_col, w_in, b_gate, q_norm_g, k_norm_g, w_attn_out,
                w_pool_mix, pool_scale, w_pool_out, w_dw, b_dw, conv_ln_g, conv_ln_b,
                w_conv_out, w_out):
    b, s, _ = x.shape
    proj = x @ w_in
    o1 = ATTN_W
    o2 = o1 + KV_W
    o3 = o2 + KV_W
    o4 = o3 + POOL_W
    o5 = o4 + CONV_W
    o6 = o5 + CONV_W
    q = proj[..., :o1].reshape(b, s, N_HEADS, HEAD_DIM)
    k = proj[..., o1:o2].reshape(b, s, N_KV_HEADS, HEAD_DIM)
    v = proj[..., o2:o3].reshape(b, s, N_KV_HEADS, HEAD_DIM)
    pool_in = proj[..., o3:o4]
    conv_a = proj[..., o4:o5]
    conv_b = proj[..., o5:o6]
    gates = jax.nn.sigmoid((proj[..., o6:] + b_gate).astype(jnp.float32)).astype(x.dtype)
    gates = gates.reshape(b, s, N_BRANCH, D_MODEL)
    q = apply_axial_rope(rms_norm_f32(q, q_norm_g), ang_row, ang_col).astype(x.dtype)
    k = apply_axial_rope(rms_norm_f32(k, k_norm_g), ang_row, ang_col).astype(x.dtype)
    attn = gqa_attention(q, k, v) @ w_attn_out
    pooled = multiscale_pool(pool_in).astype(x.dtype)
    pool = (jnp.einsum('bsgc,gcd->bsgd', pooled, w_pool_mix).reshape(b, s, POOL_W) * pool_scale) @ w_pool_out
    conv = conformer_conv(conv_a, conv_b, w_dw, b_dw, conv_ln_g, conv_ln_b) @ w_conv_out
    merged = gates[:, :, 0] * attn + gates[:, :, 1] * pool + gates[:, :, 2] * conv
    return merged @ w_out


def swiglu(x, w_g, w_u, w_d):
    return (jax.nn.silu(x @ w_g) * (x @ w_u)) @ w_d


def moe_swiglu(x, w_router, b_router, w_g, w_u, w_d):
    logits = (x @ w_router).astype(jnp.float32) + b_router.astype(jnp.float32)
    top_v, top_i = lax.top_k(logits, TOP_K)
    top_w = jax.nn.softmax(top_v, axis=-1)
    combine = jnp.sum(jax.nn.one_hot(top_i, N_EXPERTS, dtype=jnp.float32) * top_w[..., None], axis=-2)
    combine = combine.astype(x.dtype)
    y = jnp.zeros_like(x)
    for e in range(N_EXPERTS):
        y = y + combine[..., e:e + 1] * swiglu(x, w_g[e], w_u[e], w_d[e])
    return y


def encoder_trunk(x, p):
    s = x.shape[1]
    rows = s // GRID_W
    ang_row, ang_col = axial_rope_angles(rows)
    for l in range(DEPTH):
        mix = token_mixer(x, ang_row, ang_col, p['w_in'][l], p['b_gate'][l], p['q_norm_g'][l], p['k_norm_g'][l],
                          p['w_attn_out'][l], p['w_pool_mix'][l], p['pool_scale'][l], p['w_pool_out'][l],
                          p['w_dw'][l], p['b_dw'][l], p['conv_ln_g'][l], p['conv_ln_b'][l],
                          p['w_conv_out'][l], p['w_out'][l])
        x = layer_norm(ALPHA * x + mix, p['ln1_g'][l], p['ln1_b'][l])
        j = l // 2
        if l % 2 == 0:
            f = swiglu(x, p['w_ff_gate'][j], p['w_ff_up'][j], p['w_ff_down'][j])
        else:
            f = moe_swiglu(x, p['w_router'][j], p['b_router'][j], p['w_e_gate'][j], p['w_e_up'][j], p['w_e_down'][j])
        x = layer_norm(ALPHA * x + f, p['ln2_g'][l], p['ln2_b'][l])
    return x


def setup_inputs(seed: int = 0) -> dict:
    key = jax.random.key(seed)
    ks = jax.random.split(key, 32)
    f32 = jnp.float32

    def nrm(k, shape, scale):
        return jax.random.normal(k, shape, f32) * scale

    def gain(k, shape):
        return 1.0 + 0.02 * jax.random.normal(k, shape, f32)

    return {
        "x_prompt": nrm(ks[0], (BATCH, SEQ, D_MODEL), 1.0),
        "x_sample": nrm(ks[1], (DEC_BATCH, DEC_SEQ, D_MODEL), 1.0),
        "w_in": nrm(ks[2], (DEPTH, D_MODEL, PROJ_W), D_MODEL ** -0.5),
        "b_gate": nrm(ks[3], (DEPTH, N_BRANCH * D_MODEL), 0.02),
        "q_norm_g": gain(ks[4], (DEPTH, HEAD_DIM)),
        "k_norm_g": gain(ks[5], (DEPTH, HEAD_DIM)),
        "w_attn_out": nrm(ks[6], (DEPTH, ATTN_W, D_MODEL), ATTN_W ** -0.5),
        "w_pool_mix": nrm(ks[7], (DEPTH, len(POOL_WINDOWS), POOL_GROUP, POOL_GROUP), POOL_GROUP ** -0.5),
        "pool_scale": gain(ks[8], (DEPTH, POOL_W)),
        "w_pool_out": nrm(ks[9], (DEPTH, POOL_W, D_MODEL), POOL_W ** -0.5),
        "w_dw": nrm(ks[10], (DEPTH, CONV_K, CONV_W), CONV_K ** -0.5),
        "b_dw": nrm(ks[11], (DEPTH, CONV_W), 0.02),
        "conv_ln_g": gain(ks[12], (DEPTH, CONV_W)),
        "conv_ln_b": nrm(ks[13], (DEPTH, CONV_W), 0.02),
        "w_conv_out": nrm(ks[14], (DEPTH, CONV_W, D_MODEL), CONV_W ** -0.5),
        "w_out": nrm(ks[15], (DEPTH, D_MODEL, D_MODEL), BETA * D_MODEL ** -0.5),
        "ln1_g": gain(ks[16], (DEPTH, D_MODEL)),
        "ln1_b": nrm(ks[17], (DEPTH, D_MODEL), 0.02),
        "w_ff_gate": nrm(ks[18], (N_DENSE, D_MODEL, D_FF), D_MODEL ** -0.5),
        "w_ff_up": nrm(ks[19], (N_DENSE, D_MODEL, D_FF), D_MODEL ** -0.5),
        "w_ff_down": nrm(ks[20], (N_DENSE, D_FF, D_MODEL), BETA * D_FF ** -0.5),
        "w_router": nrm(ks[21], (N_MOE, D_MODEL, N_EXPERTS), D_MODEL ** -0.5),
        "b_router": nrm(ks[22], (N_MOE, N_EXPERTS), 0.01),
        "w_e_gate": nrm(ks[23], (N_MOE, N_EXPERTS, D_MODEL, D_FF_EXPERT), D_MODEL ** -0.5),
        "w_e_up": nrm(ks[24], (N_MOE, N_EXPERTS, D_MODEL, D_FF_EXPERT), D_MODEL ** -0.5),
        "w_e_down": nrm(ks[25], (N_MOE, N_EXPERTS, D_FF_EXPERT, D_MODEL), BETA * D_FF_EXPERT ** -0.5),
        "ln2_g": gain(ks[26], (DEPTH, D_MODEL)),
        "ln2_b": nrm(ks[27], (DEPTH, D_MODEL), 0.02),
    }


def reference(x_prompt, x_sample, w_in, b_gate, q_norm_g, k_norm_g, w_attn_out, w_pool_mix, pool_scale,
              w_pool_out, w_dw, b_dw, conv_ln_g, conv_ln_b, w_conv_out, w_out, ln1_g, ln1_b,
              w_ff_gate, w_ff_up, w_ff_down, w_router, b_router, w_e_gate, w_e_up, w_e_down, ln2_g, ln2_b):
    params = dict(w_in=w_in, b_gate=b_gate, q_norm_g=q_norm_g, k_norm_g=k_norm_g, w_attn_out=w_attn_out,
                  w_pool_mix=w_pool_mix, pool_scale=pool_scale, w_pool_out=w_pool_out, w_dw=w_dw, b_dw=b_dw,
                  conv_ln_g=conv_ln_g, conv_ln_b=conv_ln_b, w_conv_out=w_conv_out, w_out=w_out,
                  ln1_g=ln1_g, ln1_b=ln1_b, w_ff_gate=w_ff_gate, w_ff_up=w_ff_up, w_ff_down=w_ff_down,
                  w_router=w_router, b_router=b_router, w_e_gate=w_e_gate, w_e_up=w_e_up, w_e_down=w_e_down,
                  ln2_g=ln2_g, ln2_b=ln2_b)
    y_prompt = encoder_trunk(x_prompt, params)
    y_sample = encoder_trunk(x_sample, params)
    return (y_prompt, y_sample)
```

```python
import functools

import jax
import jax.numpy as jnp
from jax import lax
from jax.experimental import pallas as pl
from jax.experimental.pallas import tpu as pltpu

D_MODEL = 1024
DEPTH = 2
GRID_W = 64
N_HEADS = 8
N_KV_HEADS = 2
HEAD_DIM = 64
ATTN_W = N_HEADS * HEAD_DIM
KV_W = N_KV_HEADS * HEAD_DIM
ROPE_THETA = 10000.0
POOL_WINDOWS = (2, 4, 8, 16)
POOL_GROUP = 64
POOL_W = POOL_GROUP * len(POOL_WINDOWS)
CONV_W = 256
CONV_K = 31
N_BRANCH = 3
FRONT_W = ATTN_W + 2 * KV_W + POOL_W + 2 * CONV_W
D_FF = 2816
N_EXPERTS = 8
D_FF_EXPERT = 1408
ALPHA = (2 * DEPTH) ** 0.25
LN_EPS = 1e-5
RMS_EPS = 1e-6

LANES = 128
POOL_PAD = 8
CONV_PAD = 16
FRONT_ROWS = 256
CONV_ROWS = 128
ATTN_TQ = 256
TOKEN_TILE = 512
NEG_BIG = -1e30
VMEM_LIMIT = 56 * 1024 * 1024

F32 = jnp.float32
BF16 = jnp.bfloat16


def _sigmoid(x):
    return 1.0 / (1.0 + jnp.exp(-x))


def _layer_norm(z, g, b):
    mu = jnp.mean(z, axis=-1, keepdims=True)
    d = z - mu
    var = jnp.mean(d * d, axis=-1, keepdims=True)
    return d * lax.rsqrt(var + LN_EPS) * g + b


def _const_spec(shape):
    zeros = (0,) * len(shape)
    return pl.BlockSpec(shape, lambda *_: zeros, pipeline_mode=pl.Buffered(1))


def _segment_sumsq(x, seg):
    sq = x * x
    hi = sq.astype(BF16)
    lo = (sq - hi.astype(F32)).astype(BF16)
    return (jnp.dot(hi, seg, preferred_element_type=F32)
            + jnp.dot(lo, seg, preferred_element_type=F32))


def _norm_rope(x, seg, gain, cos, sin_a, sin_b):
    width = x.shape[-1]
    ms = _segment_sumsq(x, seg) * (1.0 / HEAD_DIM)
    xn = x * lax.rsqrt(ms + RMS_EPS) * gain
    quarter = HEAD_DIM // 4
    up = pltpu.roll(xn, width - quarter, 1)
    down = pltpu.roll(xn, quarter, 1)
    return xn * cos + up * sin_a + down * sin_b


def _front_kernel(x_ref, w_ref, cos_ref, sina_ref, sinb_ref, qg_ref, kg_ref, seg_ref, wmix_ref,
                  pscale_ref, wdw_ref, bdw_ref, lng_ref, lnb_ref,
                  q_ref, k_ref, v_ref, pool_ref, conv_ref, ubuf, hbuf):
    seq = x_ref.shape[1]
    rows = min(FRONT_ROWS, seq)
    o1, o2, o3 = ATTN_W, ATTN_W + KV_W, ATTN_W + 2 * KV_W
    o4, o5 = o3 + POOL_W, o3 + POOL_W + CONV_W

    ubuf[0:POOL_PAD, :] = jnp.zeros((POOL_PAD, POOL_W), F32)
    ubuf[POOL_PAD + seq:, :] = jnp.zeros((POOL_PAD, POOL_W), F32)
    hbuf[0:CONV_PAD, :] = jnp.zeros((CONV_PAD, CONV_W), F32)
    hbuf[CONV_PAD + seq:, :] = jnp.zeros((CONV_PAD, CONV_W), F32)

    def proj_chunk(c, carry):
        r0 = pl.multiple_of(c * rows, rows)
        xb = x_ref[0, pl.ds(r0, rows), :].astype(BF16)
        proj = jnp.dot(xb, w_ref[...], preferred_element_type=F32)
        cos = cos_ref[pl.ds(r0, rows), :]
        sin_a = sina_ref[pl.ds(r0, rows), :]
        sin_b = sinb_ref[pl.ds(r0, rows), :]
        rep = ATTN_W // LANES
        q = _norm_rope(proj[:, :o1], seg_ref[...], qg_ref[...],
                       jnp.concatenate([cos] * rep, axis=1),
                       jnp.concatenate([sin_a] * rep, axis=1),
                       jnp.concatenate([sin_b] * rep, axis=1))
        q_ref[0, pl.ds(r0, rows), :] = (q * (HEAD_DIM ** -0.5)).astype(BF16)
        k = _norm_rope(proj[:, o1:o2], seg_ref[0:KV_W, 0:KV_W], kg_ref[...], cos, sin_a, sin_b)
        k_ref[0, pl.ds(r0, rows), :] = k.astype(BF16)
        v_ref[0, pl.ds(r0, rows), :] = proj[:, o2:o3].astype(BF16)
        ubuf[pl.ds(POOL_PAD + r0, rows), :] = proj[:, o3:o4]
        hbuf[pl.ds(CONV_PAD + r0, rows), :] = proj[:, o4:o5] * _sigmoid(proj[:, o5:])
        return carry

    lax.fori_loop(0, seq // rows, proj_chunk, 0)

    lane = lax.broadcasted_iota(jnp.int32, (rows, LANES), 1)
    first = lane < POOL_GROUP
    for c in range(seq // rows):
        r0 = c * rows
        t = r0 + lax.broadcasted_iota(jnp.int32, (rows, LANES), 0)
        halves = []
        for half in range(POOL_W // LANES):
            w_a, w_b = POOL_WINDOWS[2 * half], POOL_WINDOWS[2 * half + 1]
            lo_a, hi_a = w_a // 2, w_a - w_a // 2 - 1
            lo_b, hi_b = w_b // 2, w_b - w_b // 2 - 1
            cols = slice(half * LANES, (half + 1) * LANES)

            def shifted(off):
                return ubuf[POOL_PAD + r0 + off:POOL_PAD + r0 + off + rows, cols]

            centre = shifted(0)
            tot_a = centre
            for off in range(-lo_a, hi_a + 1):
                if off != 0:
                    tot_a = tot_a + shifted(off)
            tot_b = tot_a
            for off in range(-lo_b, hi_b + 1):
                if off < -lo_a or off > hi_a:
                    tot_b = tot_b + shifted(off)
            lo = jnp.where(first, lo_a, lo_b)
            hi = jnp.where(first, hi_a, hi_b)
            cnt = (jnp.minimum(t + hi + 1, seq) - jnp.maximum(t - lo, 0)).astype(F32)
            tot = jnp.where(first, tot_a, tot_b)
            halves.append(tot / cnt - centre)
        pooled = jnp.concatenate(halves, axis=1).astype(BF16)
        mixed = jnp.dot(pooled, wmix_ref[...], preferred_element_type=F32) * pscale_ref[...]
        pool_ref[0, r0:r0 + rows, :] = mixed.astype(BF16)

    crow = min(CONV_ROWS, seq)
    for c in range(seq // crow):
        r0 = c * crow
        acc = jnp.broadcast_to(bdw_ref[...], (crow, CONV_W))
        for kk in range(CONV_K):
            start = CONV_PAD + r0 + kk - CONV_K // 2
            acc = acc + hbuf[start:start + crow, :] * wdw_ref[kk:kk + 1, :]
        hn = _layer_norm(acc, lng_ref[...], lnb_ref[...])
        conv_ref[0, r0:r0 + crow, :] = (hn * _sigmoid(hn)).astype(BF16)


def _front_call(x, fw):
    batch, seq, _ = x.shape
    seq_spec = lambda w: pl.BlockSpec((1, seq, w), lambda b: (b, 0, 0))
    out_shape = (
        jax.ShapeDtypeStruct((batch, seq, ATTN_W), BF16),
        jax.ShapeDtypeStruct((batch, seq, KV_W), BF16),
        jax.ShapeDtypeStruct((batch, seq, KV_W), BF16),
        jax.ShapeDtypeStruct((batch, seq, POOL_W), BF16),
        jax.ShapeDtypeStruct((batch, seq, CONV_W), BF16),
    )
    consts = (fw["w_front"], fw["cos"], fw["sin_a"], fw["sin_b"], fw["q_gain"], fw["k_gain"],
              fw["seg"], fw["w_mix"], fw["pool_scale"], fw["w_dw"], fw["b_dw"], fw["conv_ln_g"],
              fw["conv_ln_b"])
    return pl.pallas_call(
        _front_kernel,
        grid=(batch,),
        in_specs=[seq_spec(D_MODEL)] + [_const_spec(c.shape) for c in consts],
        out_specs=(seq_spec(ATTN_W), seq_spec(KV_W), seq_spec(KV_W), seq_spec(POOL_W),
                   seq_spec(CONV_W)),
        out_shape=out_shape,
        scratch_shapes=[pltpu.VMEM((seq + 2 * POOL_PAD, POOL_W), F32),
                        pltpu.VMEM((seq + 2 * CONV_PAD, CONV_W), F32)],
        compiler_params=pltpu.CompilerParams(dimension_semantics=("arbitrary",),
                                             vmem_limit_bytes=VMEM_LIMIT),
        name="front",
    )(x, *consts)


def _attn_kernel(q_ref, k_ref, v_ref, o_ref):
    group = N_HEADS // N_KV_HEADS
    for kv in range(N_KV_HEADS):
        kcols = slice(kv * HEAD_DIM, (kv + 1) * HEAD_DIM)
        k = k_ref[0, :, kcols]
        v = v_ref[0, :, kcols]
        for g in range(group):
            hcols = slice((kv * group + g) * HEAD_DIM, (kv * group + g + 1) * HEAD_DIM)
            q = q_ref[0, :, hcols]
            s = lax.dot_general(q, k, (((1,), (1,)), ((), ())), preferred_element_type=F32)
            m = jnp.max(s, axis=-1, keepdims=True)
            p = jnp.exp(s - m)
            denom = jnp.sum(p, axis=-1, keepdims=True)
            o = jnp.dot(p.astype(BF16), v, preferred_element_type=F32) / denom
            o_ref[0, :, hcols] = o.astype(BF16)


def _attn_call(q, k, v):
    batch, seq, _ = q.shape
    tq = min(ATTN_TQ, seq)
    return pl.pallas_call(
        _attn_kernel,
        grid=(batch, seq // tq),
        in_specs=[pl.BlockSpec((1, tq, ATTN_W), lambda b, i: (b, i, 0)),
                  pl.BlockSpec((1, seq, KV_W), lambda b, i: (b, 0, 0)),
                  pl.BlockSpec((1, seq, KV_W), lambda b, i: (b, 0, 0))],
        out_specs=pl.BlockSpec((1, tq, ATTN_W), lambda b, i: (b, i, 0)),
        out_shape=jax.ShapeDtypeStruct((batch, seq, ATTN_W), BF16),
        compiler_params=pltpu.CompilerParams(dimension_semantics=("arbitrary", "arbitrary"),
                                             vmem_limit_bytes=VMEM_LIMIT),
        name="attention",
    )(q, k, v)


def _merge_kernel(x_ref, a_ref, p_ref, c_ref, wg_ref, bg_ref, wa_ref, wp_ref, wc_ref, wo_ref,
                  g_ref, b_ref, o_ref):
    x = x_ref[...]
    xb = x.astype(BF16)
    merged = None
    for br, (in_ref, w_ref) in enumerate(((a_ref, wa_ref), (p_ref, wp_ref), (c_ref, wc_ref))):
        cols = slice(br * D_MODEL, (br + 1) * D_MODEL)
        gate = _sigmoid(jnp.dot(xb, wg_ref[:, cols], preferred_element_type=F32) + bg_ref[:, cols])
        term = gate * jnp.dot(in_ref[...], w_ref[...], preferred_element_type=F32)
        merged = term if merged is None else merged + term
    mix = jnp.dot(merged.astype(BF16), wo_ref[...], preferred_element_type=F32)
    o_ref[...] = _layer_norm(ALPHA * x + mix, g_ref[...], b_ref[...])


def _merge_call(x, attn, pool, conv, mw):
    n = x.shape[0]
    tm = min(TOKEN_TILE, n)
    row_spec = lambda w: pl.BlockSpec((tm, w), lambda i: (i, 0))
    consts = (mw["w_gate"], mw["b_gate"], mw["w_attn_out"], mw["w_pool_out"], mw["w_conv_out"],
              mw["w_out"], mw["ln1_g"], mw["ln1_b"])
    return pl.pallas_call(
        _merge_kernel,
        grid=(n // tm,),
        in_specs=[row_spec(D_MODEL), row_spec(ATTN_W), row_spec(POOL_W), row_spec(CONV_W)]
        + [_const_spec(c.shape) for c in consts],
        out_specs=row_spec(D_MODEL),
        out_shape=jax.ShapeDtypeStruct((n, D_MODEL), F32),
        compiler_params=pltpu.CompilerParams(dimension_semantics=("arbitrary",),
                                             vmem_limit_bytes=VMEM_LIMIT),
        name="merge",
    )(x, attn, pool, conv, *consts)


def _swiglu_partial(xb, wg, wu, wd):
    hg = jnp.dot(xb, wg, preferred_element_type=F32)
    hu = jnp.dot(xb, wu, preferred_element_type=F32)
    h = (hg * _sigmoid(hg) * hu).astype(BF16)
    return jnp.dot(h, wd, preferred_element_type=F32)


def _ffn_kernel(x_ref, wg_ref, wu_ref, wd_ref, g_ref, b_ref, o_ref):
    x = x_ref[...]
    xb = x.astype(BF16)
    half = D_FF // 2
    f = None
    for c in range(2):
        cols = slice(c * half, (c + 1) * half)
        part = _swiglu_partial(xb, wg_ref[:, cols], wu_ref[:, cols], wd_ref[cols, :])
        f = part if f is None else f + part
    o_ref[...] = _layer_norm(ALPHA * x + f, g_ref[...], b_ref[...])


def _ffn_call(x, dw):
    n = x.shape[0]
    tm = min(TOKEN_TILE, n)
    row_spec = pl.BlockSpec((tm, D_MODEL), lambda i: (i, 0))
    consts = (dw["w_gate"], dw["w_up"], dw["w_down"], dw["ln2_g"], dw["ln2_b"])
    return pl.pallas_call(
        _ffn_kernel,
        grid=(n // tm,),
        in_specs=[row_spec] + [_const_spec(c.shape) for c in consts],
        out_specs=row_spec,
        out_shape=jax.ShapeDtypeStruct((n, D_MODEL), F32),
        compiler_params=pltpu.CompilerParams(dimension_semantics=("arbitrary",),
                                             vmem_limit_bytes=VMEM_LIMIT),
        name="ffn",
    )(x, *consts)


def _moe_kernel(x_ref, wr_ref, br_ref, wg_ref, wu_ref, wd_ref, g_ref, b_ref, o_ref,
                xb_ref, comb_ref, acc_ref):
    e = pl.program_id(1)
    lane = lax.broadcasted_iota(jnp.int32, comb_ref.shape, 1)

    @pl.when(e == 0)
    def _():
        xb = x_ref[...].astype(BF16)
        xb_ref[...] = xb
        logits = jnp.dot(xb, wr_ref[...], preferred_element_type=F32) + br_ref[...]
        m1 = jnp.max(logits, axis=-1, keepdims=True)
        i1 = jnp.min(jnp.where(logits == m1, lane, LANES), axis=-1, keepdims=True)
        rest = jnp.where(lane == i1, NEG_BIG, logits)
        m2 = jnp.max(rest, axis=-1, keepdims=True)
        i2 = jnp.min(jnp.where(rest == m2, lane, LANES), axis=-1, keepdims=True)
        e2 = jnp.exp(m2 - m1)
        w1 = 1.0 / (1.0 + e2)
        w2 = e2 / (1.0 + e2)
        comb_ref[...] = jnp.where(lane == i1, w1, 0.0) + jnp.where(lane == i2, w2, 0.0)
        acc_ref[...] = jnp.zeros_like(acc_ref)

    weight = jnp.sum(jnp.where(lane == e, comb_ref[...], 0.0), axis=-1, keepdims=True)
    acc_ref[...] += weight * _swiglu_partial(xb_ref[...], wg_ref[0], wu_ref[0], wd_ref[0])

    @pl.when(e == N_EXPERTS - 1)
    def _():
        o_ref[...] = _layer_norm(ALPHA * x_ref[...] + acc_ref[...], g_ref[...], b_ref[...])


def _moe_call(x, ew):
    n = x.shape[0]
    tm = min(TOKEN_TILE, n)
    row_spec = pl.BlockSpec((tm, D_MODEL), lambda i, e: (i, 0))
    up_spec = pl.BlockSpec((1, D_MODEL, D_FF_EXPERT), lambda i, e: (e, 0, 0))
    down_spec = pl.BlockSpec((1, D_FF_EXPERT, D_MODEL), lambda i, e: (e, 0, 0))
    small = (ew["ln2_g"], ew["ln2_b"])
    return pl.pallas_call(
        _moe_kernel,
        grid=(n // tm, N_EXPERTS),
        in_specs=[row_spec, _const_spec(ew["w_router"].shape), _const_spec(ew["b_router"].shape),
                  up_spec, up_spec, down_spec] + [_const_spec(c.shape) for c in small],
        out_specs=row_spec,
        out_shape=jax.ShapeDtypeStruct((n, D_MODEL), F32),
        scratch_shapes=[pltpu.VMEM((tm, D_MODEL), BF16), pltpu.VMEM((tm, LANES), F32),
                        pltpu.VMEM((tm, D_MODEL), F32)],
        compiler_params=pltpu.CompilerParams(dimension_semantics=("arbitrary", "arbitrary"),
                                             vmem_limit_bytes=VMEM_LIMIT),
        name="experts",
    )(x, ew["w_router"], ew["b_router"], ew["w_gate"], ew["w_up"], ew["w_down"], *small)


def _rope_tables(seq):
    rows = seq // GRID_W
    row = jnp.repeat(jnp.arange(rows), GRID_W).astype(F32)
    col = jnp.tile(jnp.arange(GRID_W), rows).astype(F32)
    n_freq = HEAD_DIM // 4
    inv = ROPE_THETA ** (-jnp.arange(n_freq, dtype=F32) / n_freq)
    ang = jnp.concatenate([row[:, None] * inv] * 2 + [col[:, None] * inv] * 2, axis=1)
    ang = jnp.concatenate([ang] * (LANES // HEAD_DIM), axis=1)
    first = (jnp.arange(LANES) % (2 * n_freq)) < n_freq
    cos, sin = jnp.cos(ang), jnp.sin(ang)
    return cos, jnp.where(first, -sin, 0.0), jnp.where(first, 0.0, sin)


def _row(v):
    return v.reshape(1, -1).astype(F32)


def _prepare(p, seq):
    cos, sin_a, sin_b = _rope_tables(seq)
    head = jnp.arange(ATTN_W) // HEAD_DIM
    seg = (head[:, None] == head[None, :]).astype(BF16)
    layers = []
    for l in range(DEPTH):
        w_in = p["w_in"][l]
        front = dict(
            w_front=w_in[:, :FRONT_W].astype(BF16), cos=cos, sin_a=sin_a, sin_b=sin_b,
            q_gain=_row(jnp.tile(p["q_norm_g"][l], N_HEADS)),
            k_gain=_row(jnp.tile(p["k_norm_g"][l], N_KV_HEADS)),
            seg=seg,
            w_mix=jax.scipy.linalg.block_diag(*p["w_pool_mix"][l]).astype(BF16),
            pool_scale=_row(p["pool_scale"][l]), w_dw=p["w_dw"][l].astype(F32),
            b_dw=_row(p["b_dw"][l]), conv_ln_g=_row(p["conv_ln_g"][l]),
            conv_ln_b=_row(p["conv_ln_b"][l]))
        merge = dict(
            w_gate=w_in[:, FRONT_W:].astype(BF16), b_gate=_row(p["b_gate"][l]),
            w_attn_out=p["w_attn_out"][l].astype(BF16), w_pool_out=p["w_pool_out"][l].astype(BF16),
            w_conv_out=p["w_conv_out"][l].astype(BF16), w_out=p["w_out"][l].astype(BF16),
            ln1_g=_row(p["ln1_g"][l]), ln1_b=_row(p["ln1_b"][l]))
        j = l // 2
        if l % 2 == 0:
            mixer = dict(w_gate=p["w_ff_gate"][j].astype(BF16), w_up=p["w_ff_up"][j].astype(BF16),
                         w_down=p["w_ff_down"][j].astype(BF16))
        else:
            pad = LANES - N_EXPERTS
            mixer = dict(
                w_router=jnp.pad(p["w_router"][j], ((0, 0), (0, pad))).astype(BF16),
                b_router=jnp.pad(_row(p["b_router"][j]), ((0, 0), (0, pad)),
                                 constant_values=NEG_BIG),
                w_gate=p["w_e_gate"][j].astype(BF16), w_up=p["w_e_up"][j].astype(BF16),
                w_down=p["w_e_down"][j].astype(BF16))
        mixer["ln2_g"] = _row(p["ln2_g"][l])
        mixer["ln2_b"] = _row(p["ln2_b"][l])
        layers.append((front, merge, mixer))
    return layers


def _trunk(x, layers):
    batch, seq, _ = x.shape
    for l, (front, merge, mixer) in enumerate(layers):
        q, k, v, pool, conv = _front_call(x, front)
        attn = _attn_call(q, k, v)
        flat = lambda a: a.reshape(batch * seq, a.shape[-1])
        x1 = _merge_call(flat(x), flat(attn), flat(pool), flat(conv), merge)
        x2 = _ffn_call(x1, mixer) if l % 2 == 0 else _moe_call(x1, mixer)
        x = x2.reshape(batch, seq, D_MODEL)
    return x


def kernel(x_prompt, x_sample, w_in, b_gate, q_norm_g, k_norm_g, w_attn_out, w_pool_mix, pool_scale,
           w_pool_out, w_dw, b_dw, conv_ln_g, conv_ln_b, w_conv_out, w_out, ln1_g, ln1_b,
           w_ff_gate, w_ff_up, w_ff_down, w_router, b_router, w_e_gate, w_e_up, w_e_down,
           ln2_g, ln2_b):
    params = dict(w_in=w_in, b_gate=b_gate, q_norm_g=q_norm_g, k_norm_g=k_norm_g,
                  w_attn_out=w_attn_out, w_pool_mix=w_pool_mix, pool_scale=pool_scale,
                  w_pool_out=w_pool_out, w_dw=w_dw, b_dw=b_dw, conv_ln_g=conv_ln_g,
                  conv_ln_b=conv_ln_b, w_conv_out=w_conv_out, w_out=w_out, ln1_g=ln1_g, ln1_b=ln1_b,
                  w_ff_gate=w_ff_gate, w_ff_up=w_ff_up, w_ff_down=w_ff_down, w_router=w_router,
                  b_router=b_router, w_e_gate=w_e_gate, w_e_up=w_e_up, w_e_down=w_e_down,
                  ln2_g=ln2_g, ln2_b=ln2_b)
    assert x_prompt.shape[1] == x_sample.shape[1]
    layers = _prepare(params, x_prompt.shape[1])
    return _trunk(x_prompt, layers), _trunk(x_sample, layers)
```

```python
import functools

import jax
import jax.numpy as jnp
from jax import lax
from jax.experimental import pallas as pl
from jax.experimental.pallas import tpu as pltpu

D_MODEL = 1024
DEPTH = 2
GRID_W = 64
N_HEADS = 8
N_KV_HEADS = 2
HEAD_DIM = 64
ATTN_W = N_HEADS * HEAD_DIM
KV_W = N_KV_HEADS * HEAD_DIM
ROPE_THETA = 10000.0
POOL_WINDOWS = (2, 4, 8, 16)
POOL_GROUP = 64
POOL_W = POOL_GROUP * len(POOL_WINDOWS)
CONV_W = 256
CONV_K = 31
N_BRANCH = 3
FRONT_W = ATTN_W + 2 * KV_W + POOL_W + 2 * CONV_W
D_FF = 2816
N_EXPERTS = 8
D_FF_EXPERT = 1408
ALPHA = (2 * DEPTH) ** 0.25
LN_EPS = 1e-5
RMS_EPS = 1e-6

LANES = 128
SUBLANES = 8
POOL_PAD = 8
CONV_PAD = 16
FRONT_ROWS = 256
CONV_ROWS = 64
ATTN_TQ = 512
TOKEN_TILE = 512
MOE_TILE = 2048
MOE_SUB = 256
MOE_ALIGN = 16
MOE_CHUNK = 256
MOE_TAIL = 128
NEG_BIG = -1e30
VMEM_LIMIT = 56 * 1024 * 1024
MOE_VMEM_LIMIT = 60 * 1024 * 1024

F32 = jnp.float32
BF16 = jnp.bfloat16


def _sigmoid(x):
    return 1.0 / (1.0 + jnp.exp(-x))


def _layer_norm(z, g, b):
    mu = jnp.mean(z, axis=-1, keepdims=True)
    d = z - mu
    var = jnp.mean(d * d, axis=-1, keepdims=True)
    return d * lax.rsqrt(var + LN_EPS) * g + b


def _const_spec(shape):
    zeros = (0,) * len(shape)
    return pl.BlockSpec(shape, lambda *_: zeros, pipeline_mode=pl.Buffered(1))


def _segment_sumsq(x, seg):
    sq = x * x
    hi = sq.astype(BF16)
    lo = (sq - hi.astype(F32)).astype(BF16)
    return (jnp.dot(hi, seg, preferred_element_type=F32)
            + jnp.dot(lo, seg, preferred_element_type=F32))


def _norm_rope(x, seg, gain, cos, sin_a, sin_b):
    width = x.shape[-1]
    ms = _segment_sumsq(x, seg) * (1.0 / HEAD_DIM)
    xn = x * lax.rsqrt(ms + RMS_EPS) * gain
    quarter = HEAD_DIM // 4
    up = pltpu.roll(xn, width - quarter, 1)
    down = pltpu.roll(xn, quarter, 1)
    return xn * cos + up * sin_a + down * sin_b


def _front_kernel(x_ref, w_ref, cos_ref, sina_ref, sinb_ref, qg_ref, kg_ref, seg_ref, wmix_ref,
                  pscale_ref, wdw_ref, bdw_ref, lng_ref, lnb_ref,
                  q_ref, k_ref, v_ref, pool_ref, conv_ref, ubuf, hbuf):
    seq = x_ref.shape[1]
    rows = min(FRONT_ROWS, seq)
    o1, o2, o3 = ATTN_W, ATTN_W + KV_W, ATTN_W + 2 * KV_W
    o4, o5 = o3 + POOL_W, o3 + POOL_W + CONV_W

    ubuf[0:POOL_PAD, :] = jnp.zeros((POOL_PAD, POOL_W), F32)
    ubuf[POOL_PAD + seq:, :] = jnp.zeros((POOL_PAD, POOL_W), F32)
    hbuf[0:CONV_PAD, :] = jnp.zeros((CONV_PAD, CONV_W), F32)
    hbuf[CONV_PAD + seq:, :] = jnp.zeros((CONV_PAD, CONV_W), F32)

    def proj_chunk(c, carry):
        r0 = pl.multiple_of(c * rows, rows)
        xb = x_ref[0, pl.ds(r0, rows), :].astype(BF16)
        proj = jnp.dot(xb, w_ref[...], preferred_element_type=F32)
        cos = cos_ref[pl.ds(r0, rows), :]
        sin_a = sina_ref[pl.ds(r0, rows), :]
        sin_b = sinb_ref[pl.ds(r0, rows), :]
        rep = ATTN_W // LANES
        q = _norm_rope(proj[:, :o1], seg_ref[...], qg_ref[...],
                       jnp.concatenate([cos] * rep, axis=1),
                       jnp.concatenate([sin_a] * rep, axis=1),
                       jnp.concatenate([sin_b] * rep, axis=1))
        q_ref[0, pl.ds(r0, rows), :] = (q * (HEAD_DIM ** -0.5)).astype(BF16)
        k = _norm_rope(proj[:, o1:o2], seg_ref[0:KV_W, 0:KV_W], kg_ref[...], cos, sin_a, sin_b)
        k_ref[0, pl.ds(r0, rows), :] = k.astype(BF16)
        v_ref[0, pl.ds(r0, rows), :] = proj[:, o2:o3].astype(BF16)
        ubuf[pl.ds(POOL_PAD + r0, rows), :] = proj[:, o3:o4]
        hbuf[pl.ds(CONV_PAD + r0, rows), :] = proj[:, o4:o5] * _sigmoid(proj[:, o5:])
        return carry

    lax.fori_loop(0, seq // rows, proj_chunk, 0)

    lane = lax.broadcasted_iota(jnp.int32, (rows, LANES), 1)
    first = lane < POOL_GROUP
    for c in range(seq // rows):
        r0 = c * rows
        t = r0 + lax.broadcasted_iota(jnp.int32, (rows, LANES), 0)
        halves = []
        for half in range(POOL_W // LANES):
            w_a, w_b = POOL_WINDOWS[2 * half], POOL_WINDOWS[2 * half + 1]
            lo_a, hi_a = w_a // 2, w_a - w_a // 2 - 1
            lo_b, hi_b = w_b // 2, w_b - w_b // 2 - 1
            cols = slice(half * LANES, (half + 1) * LANES)

            def shifted(off):
                return ubuf[POOL_PAD + r0 + off:POOL_PAD + r0 + off + rows, cols]

            centre = shifted(0)
            tot_a = centre
            for off in range(-lo_a, hi_a + 1):
                if off != 0:
                    tot_a = tot_a + shifted(off)
            tot_b = tot_a
            for off in range(-lo_b, hi_b + 1):
                if off < -lo_a or off > hi_a:
                    tot_b = tot_b + shifted(off)
            lo = jnp.where(first, lo_a, lo_b)
            hi = jnp.where(first, hi_a, hi_b)
            cnt = (jnp.minimum(t + hi + 1, seq) - jnp.maximum(t - lo, 0)).astype(F32)
            tot = jnp.where(first, tot_a, tot_b)
            halves.append(tot / cnt - centre)
        pooled = jnp.concatenate(halves, axis=1).astype(BF16)
        mixed = jnp.dot(pooled, wmix_ref[...], preferred_element_type=F32) * pscale_ref[...]
        pool_ref[0, r0:r0 + rows, :] = mixed.astype(BF16)

    crow = min(CONV_ROWS, seq)
    n_win = -(-CONV_K // SUBLANES)
    shift0 = CONV_PAD - CONV_K // 2

    def conv_chunk(c, carry):
        r0 = pl.multiple_of(c * crow, crow)
        halves = []
        for half in range(CONV_W // LANES):
            cols = slice(half * LANES, (half + 1) * LANES)
            acc = jnp.broadcast_to(bdw_ref[:, cols], (crow, LANES))
            for s in range(SUBLANES):
                g = None
                for a in range(n_win):
                    kk = SUBLANES * a + s
                    if kk < CONV_K:
                        win = hbuf[pl.ds(r0 + SUBLANES * a, crow + SUBLANES), cols]
                        term = win * wdw_ref[kk:kk + 1, cols]
                        g = term if g is None else g + term
                acc = acc + g[s + shift0:s + shift0 + crow, :]
            halves.append(acc)
        hn = _layer_norm(jnp.concatenate(halves, axis=1), lng_ref[...], lnb_ref[...])
        conv_ref[0, pl.ds(r0, crow), :] = (hn * _sigmoid(hn)).astype(BF16)
        return carry

    lax.fori_loop(0, seq // crow, conv_chunk, 0)


def _front_call(x, fw):
    batch, seq, _ = x.shape
    seq_spec = lambda w: pl.BlockSpec((1, seq, w), lambda b: (b, 0, 0))
    out_shape = (
        jax.ShapeDtypeStruct((batch, seq, ATTN_W), BF16),
        jax.ShapeDtypeStruct((batch, seq, KV_W), BF16),
        jax.ShapeDtypeStruct((batch, seq, KV_W), BF16),
        jax.ShapeDtypeStruct((batch, seq, POOL_W), BF16),
        jax.ShapeDtypeStruct((batch, seq, CONV_W), BF16),
    )
    consts = (fw["w_front"], fw["cos"], fw["sin_a"], fw["sin_b"], fw["q_gain"], fw["k_gain"],
              fw["seg"], fw["w_mix"], fw["pool_scale"], fw["w_dw"], fw["b_dw"], fw["conv_ln_g"],
              fw["conv_ln_b"])
    return pl.pallas_call(
        _front_kernel,
        grid=(batch,),
        in_specs=[seq_spec(D_MODEL)] + [_const_spec(c.shape) for c in consts],
        out_specs=(seq_spec(ATTN_W), seq_spec(KV_W), seq_spec(KV_W), seq_spec(POOL_W),
                   seq_spec(CONV_W)),
        out_shape=out_shape,
        scratch_shapes=[pltpu.VMEM((seq + 2 * POOL_PAD, POOL_W), F32),
                        pltpu.VMEM((seq + 2 * CONV_PAD, CONV_W), F32)],
        compiler_params=pltpu.CompilerParams(dimension_semantics=("arbitrary",),
                                             vmem_limit_bytes=VMEM_LIMIT),
        name="front",
    )(x, *consts)


def _attn_kernel(q_ref, k_ref, v_ref, o_ref):
    group = N_HEADS // N_KV_HEADS
    for kv in range(N_KV_HEADS):
        kcols = slice(kv * HEAD_DIM, (kv + 1) * HEAD_DIM)
        k = k_ref[0, :, kcols]
        v = v_ref[0, :, kcols]
        for g in range(group):
            hcols = slice((kv * group + g) * HEAD_DIM, (kv * group + g + 1) * HEAD_DIM)
            q = q_ref[0, :, hcols]
            s = lax.dot_general(q, k, (((1,), (1,)), ((), ())), preferred_element_type=F32)
            m = jnp.max(s, axis=-1, keepdims=True)
            p = jnp.exp(s - m)
            denom = jnp.sum(p, axis=-1, keepdims=True)
            o = jnp.dot(p.astype(BF16), v, preferred_element_type=F32) / denom
            o_ref[0, :, hcols] = o.astype(BF16)


def _attn_call(q, k, v):
    batch, seq, _ = q.shape
    tq = min(ATTN_TQ, seq)
    return pl.pallas_call(
        _attn_kernel,
        grid=(batch, seq // tq),
        in_specs=[pl.BlockSpec((1, tq, ATTN_W), lambda b, i: (b, i, 0)),
                  pl.BlockSpec((1, seq, KV_W), lambda b, i: (b, 0, 0)),
                  pl.BlockSpec((1, seq, KV_W), lambda b, i: (b, 0, 0))],
        out_specs=pl.BlockSpec((1, tq, ATTN_W), lambda b, i: (b, i, 0)),
        out_shape=jax.ShapeDtypeStruct((batch, seq, ATTN_W), BF16),
        compiler_params=pltpu.CompilerParams(dimension_semantics=("arbitrary", "arbitrary"),
                                             vmem_limit_bytes=VMEM_LIMIT),
        name="attention",
    )(q, k, v)


def _merge_kernel(x_ref, a_ref, p_ref, c_ref, wg_ref, bg_ref, wa_ref, wp_ref, wc_ref, wo_ref,
                  g_ref, b_ref, o_ref):
    x = x_ref[...]
    xb = x.astype(BF16)
    merged = None
    for br, (in_ref, w_ref) in enumerate(((a_ref, wa_ref), (p_ref, wp_ref), (c_ref, wc_ref))):
        cols = slice(br * D_MODEL, (br + 1) * D_MODEL)
        gate = _sigmoid(jnp.dot(xb, wg_ref[:, cols], preferred_element_type=F32) + bg_ref[:, cols])
        term = gate * jnp.dot(in_ref[...], w_ref[...], preferred_element_type=F32)
        merged = term if merged is None else merged + term
    mix = jnp.dot(merged.astype(BF16), wo_ref[...], preferred_element_type=F32)
    o_ref[...] = _layer_norm(ALPHA * x + mix, g_ref[...], b_ref[...])


def _merge_call(x, attn, pool, conv, mw):
    n = x.shape[0]
    tm = min(TOKEN_TILE, n)
    row_spec = lambda w: pl.BlockSpec((tm, w), lambda i: (i, 0))
    consts = (mw["w_gate"], mw["b_gate"], mw["w_attn_out"], mw["w_pool_out"], mw["w_conv_out"],
              mw["w_out"], mw["ln1_g"], mw["ln1_b"])
    return pl.pallas_call(
        _merge_kernel,
        grid=(n // tm,),
        in_specs=[row_spec(D_MODEL), row_spec(ATTN_W), row_spec(POOL_W), row_spec(CONV_W)]
        + [_const_spec(c.shape) for c in consts],
        out_specs=row_spec(D_MODEL),
        out_shape=jax.ShapeDtypeStruct((n, D_MODEL), F32),
        compiler_params=pltpu.CompilerParams(dimension_semantics=("arbitrary",),
                                             vmem_limit_bytes=VMEM_LIMIT),
        name="merge",
    )(x, attn, pool, conv, *consts)


def _swiglu_partial(xb, wg, wu, wd):
    hg = jnp.dot(xb, wg, preferred_element_type=F32)
    hu = jnp.dot(xb, wu, preferred_element_type=F32)
    h = (hg * _sigmoid(hg) * hu).astype(BF16)
    return jnp.dot(h, wd, preferred_element_type=F32)


def _ffn_kernel(x_ref, wg_ref, wu_ref, wd_ref, g_ref, b_ref, o_ref):
    x = x_ref[...]
    xb = x.astype(BF16)
    half = D_FF // 2
    f = None
    for c in range(2):
        cols = slice(c * half, (c + 1) * half)
        part = _swiglu_partial(xb, wg_ref[:, cols], wu_ref[:, cols], wd_ref[cols, :])
        f = part if f is None else f + part
    o_ref[...] = _layer_norm(ALPHA * x + f, g_ref[...], b_ref[...])


def _ffn_call(x, dw):
    n = x.shape[0]
    tm = min(TOKEN_TILE, n)
    row_spec = pl.BlockSpec((tm, D_MODEL), lambda i: (i, 0))
    consts = (dw["w_gate"], dw["w_up"], dw["w_down"], dw["ln2_g"], dw["ln2_b"])
    return pl.pallas_call(
        _ffn_kernel,
        grid=(n // tm,),
        in_specs=[row_spec] + [_const_spec(c.shape) for c in consts],
        out_specs=row_spec,
        out_shape=jax.ShapeDtypeStruct((n, D_MODEL), F32),
        compiler_params=pltpu.CompilerParams(dimension_semantics=("arbitrary",),
                                             vmem_limit_bytes=VMEM_LIMIT),
        name="ffn",
    )(x, *consts)


def _route(logits, lane):
    m1 = jnp.max(logits, axis=-1, keepdims=True)
    i1 = jnp.min(jnp.where(logits == m1, lane, LANES), axis=-1, keepdims=True)
    rest = jnp.where(lane == i1, NEG_BIG, logits)
    m2 = jnp.max(rest, axis=-1, keepdims=True)
    i2 = jnp.min(jnp.where(rest == m2, lane, LANES), axis=-1, keepdims=True)
    e2 = jnp.exp(m2 - m1)
    w1 = 1.0 / (1.0 + e2)
    w2 = e2 / (1.0 + e2)
    mask = jnp.where(lane == i1, 1.0, jnp.where(lane == i2, 1.0, 0.0))
    comb = jnp.where(lane == i1, w1, jnp.where(lane == i2, w2, 0.0))
    return mask, comb


def _moe_kernel(x_ref, wr_ref, br_ref, tri_ref, wg_ref, wu_ref, wd_ref, g_ref, b_ref, o_ref,
                comb_ref, rank_ref, rankt_ref, cnt_ref, offs_ref, xy_ref):
    tile = x_ref.shape[0]
    n_sub = tile // MOE_SUB
    e = pl.program_id(1)
    lane = lax.broadcasted_iota(jnp.int32, (MOE_SUB, LANES), 1)
    slot_rows = lax.broadcasted_iota(jnp.int32, (MOE_SUB, MOE_SUB), 0).astype(F32)
    slot_cols = lax.broadcasted_iota(jnp.int32, (MOE_SUB, MOE_SUB), 1).astype(F32)

    @pl.when(jnp.logical_and(pl.program_id(0) == 0, e == 0))
    def _():
        xy_ref[...] = jnp.zeros_like(xy_ref)

    @pl.when(e == 0)
    def _():
        for sb in range(n_sub):
            rows = slice(sb * MOE_SUB, (sb + 1) * MOE_SUB)
            xb = x_ref[rows, :].astype(BF16)
            logits = jnp.dot(xb, wr_ref[...], preferred_element_type=F32) + br_ref[...]
            mask, comb = _route(logits, lane)
            comb_ref[rows, :] = comb
            rank = jnp.dot(tri_ref[...], mask.astype(BF16), preferred_element_type=F32)
            rank = jnp.where(mask > 0.0, rank, -1.0)
            rank_ref[rows, :] = rank
            rankt_ref[:, rows] = rank.T
            counts = jnp.sum(mask, axis=0, keepdims=True).astype(jnp.int32)
            for ee in range(N_EXPERTS):
                cnt_ref[sb * N_EXPERTS + ee] = counts[0, ee]
        o_ref[...] = jnp.zeros_like(o_ref)

    off = 0
    for sb in range(n_sub):
        rows = slice(sb * MOE_SUB, (sb + 1) * MOE_SUB)
        slot_of_token = rankt_ref[pl.ds(e, 1), rows]
        take = jnp.where(slot_rows == slot_of_token, 1.0, 0.0).astype(BF16)
        gathered = jnp.dot(take, x_ref[rows, :].astype(BF16), preferred_element_type=F32)
        xy_ref[pl.ds(pl.multiple_of(off, MOE_ALIGN), MOE_SUB), :] = gathered.astype(BF16)
        offs_ref[sb] = off
        count = cnt_ref[sb * N_EXPERTS + e]
        off = off + ((count + (MOE_ALIGN - 1)) // MOE_ALIGN) * MOE_ALIGN
    n_rows = off

    def expert_rows(start, size):
        xc = xy_ref[pl.ds(start, size), :]
        y = _swiglu_partial(xc, wg_ref[0], wu_ref[0], wd_ref[0])
        xy_ref[pl.ds(start, size), :] = y.astype(BF16)

    n_full = n_rows // MOE_CHUNK

    def full_body(i, carry):
        expert_rows(pl.multiple_of(i * MOE_CHUNK, MOE_CHUNK), MOE_CHUNK)
        return carry

    lax.fori_loop(0, n_full, full_body, 0)
    n_tail = (n_rows - n_full * MOE_CHUNK + (MOE_TAIL - 1)) // MOE_TAIL

    def tail_body(j, carry):
        expert_rows(pl.multiple_of(n_full * MOE_CHUNK + j * MOE_TAIL, MOE_TAIL), MOE_TAIL)
        return carry

    lax.fori_loop(0, n_tail, tail_body, 0)

    for sb in range(n_sub):
        rows = slice(sb * MOE_SUB, (sb + 1) * MOE_SUB)
        here = lane == e
        slot = jnp.sum(jnp.where(here, rank_ref[rows, :], 0.0), axis=-1, keepdims=True)
        weight = jnp.sum(jnp.where(here, comb_ref[rows, :], 0.0), axis=-1, keepdims=True)
        put = jnp.where(slot_cols == slot, 1.0, 0.0).astype(BF16)
        yb = xy_ref[pl.ds(pl.multiple_of(offs_ref[sb], MOE_ALIGN), MOE_SUB), :]
        o_ref[rows, :] += weight * jnp.dot(put, yb, preferred_element_type=F32)

    @pl.when(e == N_EXPERTS - 1)
    def _():
        o_ref[...] = _layer_norm(ALPHA * x_ref[...] + o_ref[...], g_ref[...], b_ref[...])


def _moe_call(x, ew):
    n = x.shape[0]
    tile = min(MOE_TILE, n)
    n_sub = tile // MOE_SUB
    row_spec = lambda **kw: pl.BlockSpec((tile, D_MODEL), lambda i, e: (i, 0), **kw)
    up_spec = pl.BlockSpec((1, D_MODEL, D_FF_EXPERT), lambda i, e: (e, 0, 0))
    down_spec = pl.BlockSpec((1, D_FF_EXPERT, D_MODEL), lambda i, e: (e, 0, 0))
    consts = (ew["w_router"], ew["b_router"], ew["tri"])
    small = (ew["ln2_g"], ew["ln2_b"])
    return pl.pallas_call(
        _moe_kernel,
        grid=(n // tile, N_EXPERTS),
        in_specs=[row_spec(pipeline_mode=pl.Buffered(1))] + [_const_spec(c.shape) for c in consts]
        + [up_spec, up_spec, down_spec] + [_const_spec(c.shape) for c in small],
        out_specs=row_spec(),
        out_shape=jax.ShapeDtypeStruct((n, D_MODEL), F32),
        scratch_shapes=[pltpu.VMEM((tile, LANES), F32), pltpu.VMEM((tile, LANES), F32),
                        pltpu.VMEM((LANES, tile), F32),
                        pltpu.SMEM((n_sub * N_EXPERTS,), jnp.int32),
                        pltpu.SMEM((n_sub,), jnp.int32),
                        pltpu.VMEM((tile + 2 * MOE_SUB, D_MODEL), BF16)],
        compiler_params=pltpu.CompilerParams(dimension_semantics=("arbitrary", "arbitrary"),
                                             vmem_limit_bytes=MOE_VMEM_LIMIT),
        name="experts",
    )(x, *consts, ew["w_gate"], ew["w_up"], ew["w_down"], *small)


def _rope_tables(seq):
    rows = seq // GRID_W
    row = jnp.repeat(jnp.arange(rows), GRID_W).astype(F32)
    col = jnp.tile(jnp.arange(GRID_W), rows).astype(F32)
    n_freq = HEAD_DIM // 4
    inv = ROPE_THETA ** (-jnp.arange(n_freq, dtype=F32) / n_freq)
    ang = jnp.concatenate([row[:, None] * inv] * 2 + [col[:, None] * inv] * 2, axis=1)
    ang = jnp.concatenate([ang] * (LANES // HEAD_DIM), axis=1)
    first = (jnp.arange(LANES) % (2 * n_freq)) < n_freq
    cos, sin = jnp.cos(ang), jnp.sin(ang)
    return cos, jnp.where(first, -sin, 0.0), jnp.where(first, 0.0, sin)


def _row(v):
    return v.reshape(1, -1).astype(F32)


def _prepare(p, seq):
    cos, sin_a, sin_b = _rope_tables(seq)
    head = jnp.arange(ATTN_W) // HEAD_DIM
    seg = (head[:, None] == head[None, :]).astype(BF16)
    layers = []
    for l in range(DEPTH):
        w_in = p["w_in"][l]
        front = dict(
            w_front=w_in[:, :FRONT_W].astype(BF16), cos=cos, sin_a=sin_a, sin_b=sin_b,
            q_gain=_row(jnp.tile(p["q_norm_g"][l], N_HEADS)),
            k_gain=_row(jnp.tile(p["k_norm_g"][l], N_KV_HEADS)),
            seg=seg,
            w_mix=jax.scipy.linalg.block_diag(*p["w_pool_mix"][l]).astype(BF16),
            pool_scale=_row(p["pool_scale"][l]), w_dw=p["w_dw"][l].astype(F32),
            b_dw=_row(p["b_dw"][l]), conv_ln_g=_row(p["conv_ln_g"][l]),
            conv_ln_b=_row(p["conv_ln_b"][l]))
        merge = dict(
            w_gate=w_in[:, FRONT_W:].astype(BF16), b_gate=_row(p["b_gate"][l]),
            w_attn_out=p["w_attn_out"][l].astype(BF16), w_pool_out=p["w_pool_out"][l].astype(BF16),
            w_conv_out=p["w_conv_out"][l].astype(BF16), w_out=p["w_out"][l].astype(BF16),
            ln1_g=_row(p["ln1_g"][l]), ln1_b=_row(p["ln1_b"][l]))
        j = l // 2
        if l % 2 == 0:
            mixer = dict(w_gate=p["w_ff_gate"][j].astype(BF16), w_up=p["w_ff_up"][j].astype(BF16),
                         w_down=p["w_ff_down"][j].astype(BF16))
        else:
            pad = LANES - N_EXPERTS
            sub = jnp.arange(MOE_SUB)
            mixer = dict(
                tri=(sub[None, :] < sub[:, None]).astype(BF16),
                w_router=jnp.pad(p["w_router"][j], ((0, 0), (0, pad))).astype(BF16),
                b_router=jnp.pad(_row(p["b_router"][j]), ((0, 0), (0, pad)),
                                 constant_values=NEG_BIG),
                w_gate=p["w_e_gate"][j].astype(BF16), w_up=p["w_e_up"][j].astype(BF16),
                w_down=p["w_e_down"][j].astype(BF16))
        mixer["ln2_g"] = _row(p["ln2_g"][l])
        mixer["ln2_b"] = _row(p["ln2_b"][l])
        layers.append((front, merge, mixer))
    return layers


def _trunk(x, layers):
    batch, seq, _ = x.shape
    for l, (front, merge, mixer) in enumerate(layers):
        q, k, v, pool, conv = _front_call(x, front)
        attn = _attn_call(q, k, v)
        flat = lambda a: a.reshape(batch * seq, a.shape[-1])
        x1 = _merge_call(flat(x), flat(attn), flat(pool), flat(conv), merge)
        x2 = _ffn_call(x1, mixer) if l % 2 == 0 else _moe_call(x1, mixer)
        x = x2.reshape(batch, seq, D_MODEL)
    return x


def kernel(x_prompt, x_sample, w_in, b_gate, q_norm_g, k_norm_g, w_attn_out, w_pool_mix, pool_scale,
           w_pool_out, w_dw, b_dw, conv_ln_g, conv_ln_b, w_conv_out, w_out, ln1_g, ln1_b,
           w_ff_gate, w_ff_up, w_ff_down, w_router, b_router, w_e_gate, w_e_up, w_e_down,
           ln2_g, ln2_b):
    params = dict(w_in=w_in, b_gate=b_gate, q_norm_g=q_norm_g, k_norm_g=k_norm_g,
                  w_attn_out=w_attn_out, w_pool_mix=w_pool_mix, pool_scale=pool_scale,
                  w_pool_out=w_pool_out, w_dw=w_dw, b_dw=b_dw, conv_ln_g=conv_ln_g,
                  conv_ln_b=conv_ln_b, w_conv_out=w_conv_out, w_out=w_out, ln1_g=ln1_g, ln1_b=ln1_b,
                  w_ff_gate=w_ff_gate, w_ff_up=w_ff_up, w_ff_down=w_ff_down, w_router=w_router,
                  b_router=b_router, w_e_gate=w_e_gate, w_e_up=w_e_up, w_e_down=w_e_down,
                  ln2_g=ln2_g, ln2_b=ln2_b)
    assert x_prompt.shape[1] == x_sample.shape[1]
    layers = _prepare(params, x_prompt.shape[1])
    return _trunk(x_prompt, layers), _trunk(x_sample, layers)
```

```python
import functools

import jax
import jax.numpy as jnp
from jax import lax
from jax.experimental import pallas as pl
from jax.experimental.pallas import tpu as pltpu

D_MODEL = 1024
DEPTH = 2
GRID_W = 64
N_HEADS = 8
N_KV_HEADS = 2
HEAD_DIM = 64
ATTN_W = N_HEADS * HEAD_DIM
KV_W = N_KV_HEADS * HEAD_DIM
ROPE_THETA = 10000.0
POOL_WINDOWS = (2, 4, 8, 16)
POOL_GROUP = 64
POOL_W = POOL_GROUP * len(POOL_WINDOWS)
CONV_W = 256
CONV_K = 31
N_BRANCH = 3
FRONT_W = ATTN_W + 2 * KV_W + POOL_W + 2 * CONV_W
D_FF = 2816
N_EXPERTS = 8
D_FF_EXPERT = 1408
ALPHA = (2 * DEPTH) ** 0.25
LN_EPS = 1e-5
RMS_EPS = 1e-6

LANES = 128
SUBLANES = 8
POOL_PAD = 8
CONV_PAD = 16
FRONT_ROWS = 256
CONV_ROWS = 64
ATTN_TQ = 512
TOKEN_TILE = 512
MOE_TILE = 2048
MOE_SUB = 256
MOE_ALIGN = 16
MOE_SLOTS = MOE_SUB // 2
MOE_CHUNK = 256
MOE_TAIL = 128
NEG_BIG = -1e30
VMEM_LIMIT = 56 * 1024 * 1024
VMEM_LIMIT_HIGH = 60 * 1024 * 1024

F32 = jnp.float32
BF16 = jnp.bfloat16


def _sigmoid(x):
    return 1.0 / (1.0 + jnp.exp(-x))


def _layer_norm(z, g, b):
    mu = jnp.mean(z, axis=-1, keepdims=True)
    d = z - mu
    var = jnp.mean(d * d, axis=-1, keepdims=True)
    return d * lax.rsqrt(var + LN_EPS) * g + b


def _const_spec(shape):
    zeros = (0,) * len(shape)
    return pl.BlockSpec(shape, lambda *_: zeros, pipeline_mode=pl.Buffered(1))


def _segment_sumsq(x, seg):
    sq = x * x
    hi = sq.astype(BF16)
    lo = (sq - hi.astype(F32)).astype(BF16)
    return (jnp.dot(hi, seg, preferred_element_type=F32)
            + jnp.dot(lo, seg, preferred_element_type=F32))


def _norm_rope(x, seg, gain, cos, sin_a, sin_b):
    width = x.shape[-1]
    ms = _segment_sumsq(x, seg) * (1.0 / HEAD_DIM)
    xn = x * lax.rsqrt(ms + RMS_EPS) * gain
    quarter = HEAD_DIM // 4
    up = pltpu.roll(xn, width - quarter, 1)
    down = pltpu.roll(xn, quarter, 1)
    return xn * cos + up * sin_a + down * sin_b


def _front_kernel(x_ref, w_ref, cos_ref, sina_ref, sinb_ref, qg_ref, kg_ref, seg_ref, wmix_ref,
                  pscale_ref, wdw_ref, bdw_ref, lng_ref, lnb_ref,
                  q_ref, k_ref, v_ref, pool_ref, conv_ref, ubuf, hbuf, hs_ref, cacc_ref):
    seq = x_ref.shape[1]
    rows = min(FRONT_ROWS, seq)
    o1, o2, o3 = ATTN_W, ATTN_W + KV_W, ATTN_W + 2 * KV_W
    o4, o5 = o3 + POOL_W, o3 + POOL_W + CONV_W

    ubuf[0:POOL_PAD, :] = jnp.zeros((POOL_PAD, POOL_W), F32)
    ubuf[POOL_PAD + seq:, :] = jnp.zeros((POOL_PAD, POOL_W), F32)
    hbuf[0:CONV_PAD, :] = jnp.zeros((CONV_PAD, CONV_W), F32)
    hbuf[CONV_PAD + seq:, :] = jnp.zeros((CONV_PAD, CONV_W), F32)

    def proj_chunk(c, carry):
        r0 = pl.multiple_of(c * rows, rows)
        xb = x_ref[0, pl.ds(r0, rows), :].astype(BF16)
        proj = jnp.dot(xb, w_ref[...], preferred_element_type=F32)
        cos = cos_ref[pl.ds(r0, rows), :]
        sin_a = sina_ref[pl.ds(r0, rows), :]
        sin_b = sinb_ref[pl.ds(r0, rows), :]
        rep = ATTN_W // LANES
        q = _norm_rope(proj[:, :o1], seg_ref[...], qg_ref[...],
                       jnp.concatenate([cos] * rep, axis=1),
                       jnp.concatenate([sin_a] * rep, axis=1),
                       jnp.concatenate([sin_b] * rep, axis=1))
        q_ref[0, pl.ds(r0, rows), :] = (q * (HEAD_DIM ** -0.5)).astype(BF16)
        k = _norm_rope(proj[:, o1:o2], seg_ref[0:KV_W, 0:KV_W], kg_ref[...], cos, sin_a, sin_b)
        k_ref[0, pl.ds(r0, rows), :] = k.astype(BF16)
        v_ref[0, pl.ds(r0, rows), :] = proj[:, o2:o3].astype(BF16)
        ubuf[pl.ds(POOL_PAD + r0, rows), :] = proj[:, o3:o4]
        hbuf[pl.ds(CONV_PAD + r0, rows), :] = proj[:, o4:o5] * _sigmoid(proj[:, o5:])
        return carry

    lax.fori_loop(0, seq // rows, proj_chunk, 0)

    lane = lax.broadcasted_iota(jnp.int32, (rows, LANES), 1)
    first = lane < POOL_GROUP
    for c in range(seq // rows):
        r0 = c * rows
        t = r0 + lax.broadcasted_iota(jnp.int32, (rows, LANES), 0)
        halves = []
        for half in range(POOL_W // LANES):
            w_a, w_b = POOL_WINDOWS[2 * half], POOL_WINDOWS[2 * half + 1]
            lo_a, hi_a = w_a // 2, w_a - w_a // 2 - 1
            lo_b, hi_b = w_b // 2, w_b - w_b // 2 - 1
            cols = slice(half * LANES, (half + 1) * LANES)

            def shifted(off):
                return ubuf[POOL_PAD + r0 + off:POOL_PAD + r0 + off + rows, cols]

            centre = shifted(0)
            tot_a = centre
            for off in range(-lo_a, hi_a + 1):
                if off != 0:
                    tot_a = tot_a + shifted(off)
            tot_b = tot_a
            for off in range(-lo_b, hi_b + 1):
                if off < -lo_a or off > hi_a:
                    tot_b = tot_b + shifted(off)
            lo = jnp.where(first, lo_a, lo_b)
            hi = jnp.where(first, hi_a, hi_b)
            cnt = (jnp.minimum(t + hi + 1, seq) - jnp.maximum(t - lo, 0)).astype(F32)
            tot = jnp.where(first, tot_a, tot_b)
            halves.append(tot / cnt - centre)
        pooled = jnp.concatenate(halves, axis=1).astype(BF16)
        mixed = jnp.dot(pooled, wmix_ref[...], preferred_element_type=F32) * pscale_ref[...]
        pool_ref[0, r0:r0 + rows, :] = mixed.astype(BF16)

    crow = min(CONV_ROWS, seq)
    shift0 = CONV_PAD - CONV_K // 2
    tail_rows = 2 * CONV_PAD - SUBLANES
    n_half = CONV_W // LANES
    for half in range(n_half):
        cols = slice(half * LANES, (half + 1) * LANES)

        def shift_chunk(c, carry):
            j0 = pl.multiple_of(c * crow, crow)
            win = hbuf[pl.ds(j0, crow + SUBLANES), cols]
            for s in range(1, SUBLANES):
                hs_ref[s - 1, pl.ds(j0, crow), :] = win[s:s + crow, :]
            return carry

        lax.fori_loop(0, seq // crow, shift_chunk, 0)
        tail = hbuf[seq:seq + 2 * CONV_PAD, cols]
        for s in range(1, SUBLANES):
            hs_ref[s - 1, seq:seq + tail_rows, :] = tail[s:s + tail_rows, :]

        def tap_chunk(c, carry):
            r0 = pl.multiple_of(c * crow, crow)
            acc = jnp.broadcast_to(bdw_ref[:, cols], (crow, LANES))
            for kk in range(CONV_K):
                a, s = divmod(kk + shift0, SUBLANES)
                start = r0 + SUBLANES * a
                src = hbuf[pl.ds(start, crow), cols] if s == 0 else hs_ref[s - 1, pl.ds(start, crow), :]
                acc = acc + src * wdw_ref[kk:kk + 1, cols]
            cacc_ref[pl.ds(r0, crow), cols] = acc
            return carry

        lax.fori_loop(0, seq // crow, tap_chunk, 0)

    def norm_chunk(c, carry):
        r0 = pl.multiple_of(c * rows, rows)
        hn = _layer_norm(cacc_ref[pl.ds(r0, rows), :], lng_ref[...], lnb_ref[...])
        conv_ref[0, pl.ds(r0, rows), :] = (hn * _sigmoid(hn)).astype(BF16)
        return carry

    lax.fori_loop(0, seq // rows, norm_chunk, 0)


def _front_call(x, fw):
    batch, seq, _ = x.shape
    seq_spec = lambda w: pl.BlockSpec((1, seq, w), lambda b: (b, 0, 0))
    out_shape = (
        jax.ShapeDtypeStruct((batch, seq, ATTN_W), BF16),
        jax.ShapeDtypeStruct((batch, seq, KV_W), BF16),
        jax.ShapeDtypeStruct((batch, seq, KV_W), BF16),
        jax.ShapeDtypeStruct((batch, seq, POOL_W), BF16),
        jax.ShapeDtypeStruct((batch, seq, CONV_W), BF16),
    )
    consts = (fw["w_front"], fw["cos"], fw["sin_a"], fw["sin_b"], fw["q_gain"], fw["k_gain"],
              fw["seg"], fw["w_mix"], fw["pool_scale"], fw["w_dw"], fw["b_dw"], fw["conv_ln_g"],
              fw["conv_ln_b"])
    return pl.pallas_call(
        _front_kernel,
        grid=(batch,),
        in_specs=[seq_spec(D_MODEL)] + [_const_spec(c.shape) for c in consts],
        out_specs=(seq_spec(ATTN_W), seq_spec(KV_W), seq_spec(KV_W), seq_spec(POOL_W),
                   seq_spec(CONV_W)),
        out_shape=out_shape,
        scratch_shapes=[pltpu.VMEM((seq + 2 * POOL_PAD, POOL_W), F32),
                        pltpu.VMEM((seq + 2 * CONV_PAD, CONV_W), F32),
                        pltpu.VMEM((SUBLANES - 1, seq + 2 * CONV_PAD, LANES), F32),
                        pltpu.VMEM((seq, CONV_W), F32)],
        compiler_params=pltpu.CompilerParams(dimension_semantics=("arbitrary",),
                                             vmem_limit_bytes=VMEM_LIMIT),
        name="front",
    )(x, *consts)


def _attn_kernel(q_ref, k_ref, v_ref, o_ref):
    group = N_HEADS // N_KV_HEADS
    for kv in range(N_KV_HEADS):
        kcols = slice(kv * HEAD_DIM, (kv + 1) * HEAD_DIM)
        k = k_ref[0, :, kcols]
        v = v_ref[0, :, kcols]
        for g in range(group):
            hcols = slice((kv * group + g) * HEAD_DIM, (kv * group + g + 1) * HEAD_DIM)
            q = q_ref[0, :, hcols]
            s = lax.dot_general(q, k, (((1,), (1,)), ((), ())), preferred_element_type=F32)
            m = jnp.max(s, axis=-1, keepdims=True)
            p = jnp.exp(s - m)
            denom = jnp.sum(p, axis=-1, keepdims=True)
            o = jnp.dot(p.astype(BF16), v, preferred_element_type=F32) / denom
            o_ref[0, :, hcols] = o.astype(BF16)


def _attn_call(q, k, v):
    batch, seq, _ = q.shape
    tq = min(ATTN_TQ, seq)
    return pl.pallas_call(
        _attn_kernel,
        grid=(batch, seq // tq),
        in_specs=[pl.BlockSpec((1, tq, ATTN_W), lambda b, i: (b, i, 0)),
                  pl.BlockSpec((1, seq, KV_W), lambda b, i: (b, 0, 0)),
                  pl.BlockSpec((1, seq, KV_W), lambda b, i: (b, 0, 0))],
        out_specs=pl.BlockSpec((1, tq, ATTN_W), lambda b, i: (b, i, 0)),
        out_shape=jax.ShapeDtypeStruct((batch, seq, ATTN_W), BF16),
        compiler_params=pltpu.CompilerParams(dimension_semantics=("arbitrary", "arbitrary"),
                                             vmem_limit_bytes=VMEM_LIMIT_HIGH),
        name="attention",
    )(q, k, v)


def _merge_kernel(x_ref, a_ref, p_ref, c_ref, wg_ref, bg_ref, wa_ref, wp_ref, wc_ref, wo_ref,
                  g_ref, b_ref, o_ref):
    x = x_ref[...]
    xb = x.astype(BF16)
    merged = None
    for br, (in_ref, w_ref) in enumerate(((a_ref, wa_ref), (p_ref, wp_ref), (c_ref, wc_ref))):
        cols = slice(br * D_MODEL, (br + 1) * D_MODEL)
        gate = _sigmoid(jnp.dot(xb, wg_ref[:, cols], preferred_element_type=F32) + bg_ref[:, cols])
        term = gate * jnp.dot(in_ref[...], w_ref[...], preferred_element_type=F32)
        merged = term if merged is None else merged + term
    mix = jnp.dot(merged.astype(BF16), wo_ref[...], preferred_element_type=F32)
    o_ref[...] = _layer_norm(ALPHA * x + mix, g_ref[...], b_ref[...])


def _merge_call(x, attn, pool, conv, mw):
    n = x.shape[0]
    tm = min(TOKEN_TILE, n)
    row_spec = lambda w: pl.BlockSpec((tm, w), lambda i: (i, 0))
    consts = (mw["w_gate"], mw["b_gate"], mw["w_attn_out"], mw["w_pool_out"], mw["w_conv_out"],
              mw["w_out"], mw["ln1_g"], mw["ln1_b"])
    return pl.pallas_call(
        _merge_kernel,
        grid=(n // tm,),
        in_specs=[row_spec(D_MODEL), row_spec(ATTN_W), row_spec(POOL_W), row_spec(CONV_W)]
        + [_const_spec(c.shape) for c in consts],
        out_specs=row_spec(D_MODEL),
        out_shape=jax.ShapeDtypeStruct((n, D_MODEL), F32),
        compiler_params=pltpu.CompilerParams(dimension_semantics=("arbitrary",),
                                             vmem_limit_bytes=VMEM_LIMIT),
        name="merge",
    )(x, attn, pool, conv, *consts)


def _swiglu_partial(xb, wg, wu, wd):
    hg = jnp.dot(xb, wg, preferred_element_type=F32)
    hu = jnp.dot(xb, wu, preferred_element_type=F32)
    h = (hg * _sigmoid(hg) * hu).astype(BF16)
    return jnp.dot(h, wd, preferred_element_type=F32)


def _ffn_kernel(x_ref, wg_ref, wu_ref, wd_ref, g_ref, b_ref, o_ref):
    x = x_ref[...]
    xb = x.astype(BF16)
    half = D_FF // 2
    f = None
    for c in range(2):
        cols = slice(c * half, (c + 1) * half)
        part = _swiglu_partial(xb, wg_ref[:, cols], wu_ref[:, cols], wd_ref[cols, :])
        f = part if f is None else f + part
    o_ref[...] = _layer_norm(ALPHA * x + f, g_ref[...], b_ref[...])


def _ffn_call(x, dw):
    n = x.shape[0]
    tm = min(TOKEN_TILE, n)
    row_spec = pl.BlockSpec((tm, D_MODEL), lambda i: (i, 0))
    consts = (dw["w_gate"], dw["w_up"], dw["w_down"], dw["ln2_g"], dw["ln2_b"])
    return pl.pallas_call(
        _ffn_kernel,
        grid=(n // tm,),
        in_specs=[row_spec] + [_const_spec(c.shape) for c in consts],
        out_specs=row_spec,
        out_shape=jax.ShapeDtypeStruct((n, D_MODEL), F32),
        compiler_params=pltpu.CompilerParams(dimension_semantics=("arbitrary",),
                                             vmem_limit_bytes=VMEM_LIMIT),
        name="ffn",
    )(x, *consts)


def _route(logits, lane):
    m1 = jnp.max(logits, axis=-1, keepdims=True)
    i1 = jnp.min(jnp.where(logits == m1, lane, LANES), axis=-1, keepdims=True)
    rest = jnp.where(lane == i1, NEG_BIG, logits)
    m2 = jnp.max(rest, axis=-1, keepdims=True)
    i2 = jnp.min(jnp.where(rest == m2, lane, LANES), axis=-1, keepdims=True)
    e2 = jnp.exp(m2 - m1)
    w1 = 1.0 / (1.0 + e2)
    w2 = e2 / (1.0 + e2)
    mask = jnp.where(lane == i1, 1.0, jnp.where(lane == i2, 1.0, 0.0))
    comb = jnp.where(lane == i1, w1, jnp.where(lane == i2, w2, 0.0))
    return mask, comb


def _moe_kernel(x_ref, wr_ref, br_ref, tri_ref, wg_ref, wu_ref, wd_ref, g_ref, b_ref, o_ref,
                comb_ref, rank_ref, rankt_ref, cnt_ref, offs_ref, xy_ref):
    tile = x_ref.shape[0]
    n_sub = tile // MOE_SUB
    e = pl.program_id(1)
    lane = lax.broadcasted_iota(jnp.int32, (MOE_SUB, LANES), 1)
    slot_rows = lax.broadcasted_iota(jnp.int32, (MOE_SLOTS, MOE_SUB), 0).astype(F32)
    slot_cols = lax.broadcasted_iota(jnp.int32, (MOE_SUB, MOE_SUB), 1).astype(F32)

    @pl.when(jnp.logical_and(pl.program_id(0) == 0, e == 0))
    def _():
        xy_ref[...] = jnp.zeros_like(xy_ref)

    @pl.when(e == 0)
    def _():
        for sb in range(n_sub):
            rows = slice(sb * MOE_SUB, (sb + 1) * MOE_SUB)
            xb = x_ref[rows, :].astype(BF16)
            logits = jnp.dot(xb, wr_ref[...], preferred_element_type=F32) + br_ref[...]
            mask, comb = _route(logits, lane)
            comb_ref[rows, :] = comb
            rank = jnp.dot(tri_ref[...], mask.astype(BF16), preferred_element_type=F32)
            rank = jnp.where(mask > 0.0, rank, -1.0)
            rank_ref[rows, :] = rank
            rankt_ref[:, rows] = rank.T
            counts = jnp.sum(mask, axis=0, keepdims=True).astype(jnp.int32)
            for ee in range(N_EXPERTS):
                cnt_ref[sb * N_EXPERTS + ee] = counts[0, ee]
        o_ref[...] = jnp.zeros_like(o_ref)

    def gather(sb, first_slot):
        rows = slice(sb * MOE_SUB, (sb + 1) * MOE_SUB)
        slot_of_token = rankt_ref[pl.ds(e, 1), rows] - first_slot
        take = jnp.where(slot_rows == slot_of_token, 1.0, 0.0).astype(BF16)
        return jnp.dot(take, x_ref[rows, :].astype(BF16), preferred_element_type=F32)

    off = 0
    for sb in range(n_sub):
        xy_ref[pl.ds(pl.multiple_of(off, MOE_ALIGN), MOE_SLOTS), :] = gather(sb, 0.0).astype(BF16)
        offs_ref[sb] = off
        count = cnt_ref[sb * N_EXPERTS + e]
        off = off + ((count + (MOE_ALIGN - 1)) // MOE_ALIGN) * MOE_ALIGN
    n_rows = off

    for sb in range(n_sub):
        count = cnt_ref[sb * N_EXPERTS + e]

        @pl.when(count > MOE_SLOTS)
        def _():
            start = pl.multiple_of(offs_ref[sb] + MOE_SLOTS, MOE_ALIGN)
            row = lax.broadcasted_iota(jnp.int32, (MOE_SLOTS, D_MODEL), 0)
            rest = gather(sb, float(MOE_SLOTS))
            keep = xy_ref[pl.ds(start, MOE_SLOTS), :].astype(F32)
            merged = jnp.where(row < count - MOE_SLOTS, rest, keep)
            xy_ref[pl.ds(start, MOE_SLOTS), :] = merged.astype(BF16)

    def expert_rows(start, size):
        xc = xy_ref[pl.ds(start, size), :]
        y = _swiglu_partial(xc, wg_ref[0], wu_ref[0], wd_ref[0])
        xy_ref[pl.ds(start, size), :] = y.astype(BF16)

    n_full = n_rows // MOE_CHUNK

    def full_body(i, carry):
        expert_rows(pl.multiple_of(i * MOE_CHUNK, MOE_CHUNK), MOE_CHUNK)
        return carry

    lax.fori_loop(0, n_full, full_body, 0)
    n_tail = (n_rows - n_full * MOE_CHUNK + (MOE_TAIL - 1)) // MOE_TAIL

    def tail_body(j, carry):
        expert_rows(pl.multiple_of(n_full * MOE_CHUNK + j * MOE_TAIL, MOE_TAIL), MOE_TAIL)
        return carry

    lax.fori_loop(0, n_tail, tail_body, 0)

    for sb in range(n_sub):
        rows = slice(sb * MOE_SUB, (sb + 1) * MOE_SUB)
        here = lane == e
        slot = jnp.sum(jnp.where(here, rank_ref[rows, :], 0.0), axis=-1, keepdims=True)
        weight = jnp.sum(jnp.where(here, comb_ref[rows, :], 0.0), axis=-1, keepdims=True)
        put = jnp.where(slot_cols == slot, 1.0, 0.0).astype(BF16)
        yb = xy_ref[pl.ds(pl.multiple_of(offs_ref[sb], MOE_ALIGN), MOE_SUB), :]
        o_ref[rows, :] += weight * jnp.dot(put, yb, preferred_element_type=F32)

    @pl.when(e == N_EXPERTS - 1)
    def _():
        o_ref[...] = _layer_norm(ALPHA * x_ref[...] + o_ref[...], g_ref[...], b_ref[...])


def _moe_call(x, ew):
    n = x.shape[0]
    tile = min(MOE_TILE, n)
    n_sub = tile // MOE_SUB
    row_spec = lambda **kw: pl.BlockSpec((tile, D_MODEL), lambda i, e: (i, 0), **kw)
    up_spec = pl.BlockSpec((1, D_MODEL, D_FF_EXPERT), lambda i, e: (e, 0, 0))
    down_spec = pl.BlockSpec((1, D_FF_EXPERT, D_MODEL), lambda i, e: (e, 0, 0))
    consts = (ew["w_router"], ew["b_router"], ew["tri"])
    small = (ew["ln2_g"], ew["ln2_b"])
    return pl.pallas_call(
        _moe_kernel,
        grid=(n // tile, N_EXPERTS),
        in_specs=[row_spec(pipeline_mode=pl.Buffered(1))] + [_const_spec(c.shape) for c in consts]
        + [up_spec, up_spec, down_spec] + [_const_spec(c.shape) for c in small],
        out_specs=row_spec(),
        out_shape=jax.ShapeDtypeStruct((n, D_MODEL), F32),
        scratch_shapes=[pltpu.VMEM((tile, LANES), F32), pltpu.VMEM((tile, LANES), F32),
                        pltpu.VMEM((LANES, tile), F32),
                        pltpu.SMEM((n_sub * N_EXPERTS,), jnp.int32),
                        pltpu.SMEM((n_sub,), jnp.int32),
                        pltpu.VMEM((tile + 2 * MOE_SUB, D_MODEL), BF16)],
        compiler_params=pltpu.CompilerParams(dimension_semantics=("arbitrary", "arbitrary"),
                                             vmem_limit_bytes=VMEM_LIMIT_HIGH),
        name="experts",
    )(x, *consts, ew["w_gate"], ew["w_up"], ew["w_down"], *small)


def _rope_tables(seq):
    rows = seq // GRID_W
    row = jnp.repeat(jnp.arange(rows), GRID_W).astype(F32)
    col = jnp.tile(jnp.arange(GRID_W), rows).astype(F32)
    n_freq = HEAD_DIM // 4
    inv = ROPE_THETA ** (-jnp.arange(n_freq, dtype=F32) / n_freq)
    ang = jnp.concatenate([row[:, None] * inv] * 2 + [col[:, None] * inv] * 2, axis=1)
    ang = jnp.concatenate([ang] * (LANES // HEAD_DIM), axis=1)
    first = (jnp.arange(LANES) % (2 * n_freq)) < n_freq
    cos, sin = jnp.cos(ang), jnp.sin(ang)
    return cos, jnp.where(first, -sin, 0.0), jnp.where(first, 0.0, sin)


def _row(v):
    return v.reshape(1, -1).astype(F32)


def _prepare(p, seq):
    cos, sin_a, sin_b = _rope_tables(seq)
    head = jnp.arange(ATTN_W) // HEAD_DIM
    seg = (head[:, None] == head[None, :]).astype(BF16)
    layers = []
    for l in range(DEPTH):
        w_in = p["w_in"][l]
        front = dict(
            w_front=w_in[:, :FRONT_W].astype(BF16), cos=cos, sin_a=sin_a, sin_b=sin_b,
            q_gain=_row(jnp.tile(p["q_norm_g"][l], N_HEADS)),
            k_gain=_row(jnp.tile(p["k_norm_g"][l], N_KV_HEADS)),
            seg=seg,
            w_mix=jax.scipy.linalg.block_diag(*p["w_pool_mix"][l]).astype(BF16),
            pool_scale=_row(p["pool_scale"][l]), w_dw=p["w_dw"][l].astype(F32),
            b_dw=_row(p["b_dw"][l]), conv_ln_g=_row(p["conv_ln_g"][l]),
            conv_ln_b=_row(p["conv_ln_b"][l]))
        merge = dict(
            w_gate=w_in[:, FRONT_W:].astype(BF16), b_gate=_row(p["b_gate"][l]),
            w_attn_out=p["w_attn_out"][l].astype(BF16), w_pool_out=p["w_pool_out"][l].astype(BF16),
            w_conv_out=p["w_conv_out"][l].astype(BF16), w_out=p["w_out"][l].astype(BF16),
            ln1_g=_row(p["ln1_g"][l]), ln1_b=_row(p["ln1_b"][l]))
        j = l // 2
        if l % 2 == 0:
            mixer = dict(w_gate=p["w_ff_gate"][j].astype(BF16), w_up=p["w_ff_up"][j].astype(BF16),
                         w_down=p["w_ff_down"][j].astype(BF16))
        else:
            pad = LANES - N_EXPERTS
            sub = jnp.arange(MOE_SUB)
            mixer = dict(
                tri=(sub[None, :] < sub[:, None]).astype(BF16),
                w_router=jnp.pad(p["w_router"][j], ((0, 0), (0, pad))).astype(BF16),
                b_router=jnp.pad(_row(p["b_router"][j]), ((0, 0), (0, pad)),
                                 constant_values=NEG_BIG),
                w_gate=p["w_e_gate"][j].astype(BF16), w_up=p["w_e_up"][j].astype(BF16),
                w_down=p["w_e_down"][j].astype(BF16))
        mixer["ln2_g"] = _row(p["ln2_g"][l])
        mixer["ln2_b"] = _row(p["ln2_b"][l])
        layers.append((front, merge, mixer))
    return layers


def _trunk(x, layers):
    batch, seq, _ = x.shape
    for l, (front, merge, mixer) in enumerate(layers):
        q, k, v, pool, conv = _front_call(x, front)
        attn = _attn_call(q, k, v)
        flat = lambda a: a.reshape(batch * seq, a.shape[-1])
        x1 = _merge_call(flat(x), flat(attn), flat(pool), flat(conv), merge)
        x2 = _ffn_call(x1, mixer) if l % 2 == 0 else _moe_call(x1, mixer)
        x = x2.reshape(batch, seq, D_MODEL)
    return x


def kernel(x_prompt, x_sample, w_in, b_gate, q_norm_g, k_norm_g, w_attn_out, w_pool_mix, pool_scale,
           w_pool_out, w_dw, b_dw, conv_ln_g, conv_ln_b, w_conv_out, w_out, ln1_g, ln1_b,
           w_ff_gate, w_ff_up, w_ff_down, w_router, b_router, w_e_gate, w_e_up, w_e_down,
           ln2_g, ln2_b):
    params = dict(w_in=w_in, b_gate=b_gate, q_norm_g=q_norm_g, k_norm_g=k_norm_g,
                  w_attn_out=w_attn_out, w_pool_mix=w_pool_mix, pool_scale=pool_scale,
                  w_pool_out=w_pool_out, w_dw=w_dw, b_dw=b_dw, conv_ln_g=conv_ln_g,
                  conv_ln_b=conv_ln_b, w_conv_out=w_conv_out, w_out=w_out, ln1_g=ln1_g, ln1_b=ln1_b,
                  w_ff_gate=w_ff_gate, w_ff_up=w_ff_up, w_ff_down=w_ff_down, w_router=w_router,
                  b_router=b_router, w_e_gate=w_e_gate, w_e_up=w_e_up, w_e_down=w_e_down,
                  ln2_g=ln2_g, ln2_b=ln2_b)
    assert x_prompt.shape[1] == x_sample.shape[1]
    layers = _prepare(params, x_prompt.shape[1])
    return _trunk(x_prompt, layers), _trunk(x_sample, layers)
```

```python
import functools

import jax
import jax.numpy as jnp
from jax import lax
from jax.experimental import pallas as pl
from jax.experimental.pallas import tpu as pltpu

D_MODEL = 1024
DEPTH = 2
GRID_W = 64
N_HEADS = 8
N_KV_HEADS = 2
HEAD_DIM = 64
ATTN_W = N_HEADS * HEAD_DIM
KV_W = N_KV_HEADS * HEAD_DIM
ROPE_THETA = 10000.0
POOL_WINDOWS = (2, 4, 8, 16)
POOL_GROUP = 64
POOL_W = POOL_GROUP * len(POOL_WINDOWS)
CONV_W = 256
CONV_K = 31
N_BRANCH = 3
FRONT_W = ATTN_W + 2 * KV_W + POOL_W + 2 * CONV_W
D_FF = 2816
N_EXPERTS = 8
D_FF_EXPERT = 1408
ALPHA = (2 * DEPTH) ** 0.25
LN_EPS = 1e-5
RMS_EPS = 1e-6

LANES = 128
SUBLANES = 8
POOL_PAD = 8
CONV_PAD = 16
FRONT_ROWS = 256
CONV_ROWS = 64
ATTN_TQ = 512
ATTN_KT = 256
ONES_ROWS = 16
LOG2_E = 1.4426950408889634
TOKEN_TILE = 512
MOE_TILE = 2048
MOE_SUB = 256
MOE_ALIGN = 16
MOE_SLOTS = MOE_SUB // 2
MOE_CHUNK = 256
MOE_TAIL = 128
NEG_BIG = -1e30
VMEM_LIMIT = 56 * 1024 * 1024
VMEM_LIMIT_HIGH = 60 * 1024 * 1024

F32 = jnp.float32
BF16 = jnp.bfloat16


def _sigmoid(x):
    return 1.0 / (1.0 + jnp.exp(-x))


def _layer_norm(z, g, b):
    mu = jnp.mean(z, axis=-1, keepdims=True)
    d = z - mu
    var = jnp.mean(d * d, axis=-1, keepdims=True)
    return d * lax.rsqrt(var + LN_EPS) * g + b


def _const_spec(shape):
    zeros = (0,) * len(shape)
    return pl.BlockSpec(shape, lambda *_: zeros, pipeline_mode=pl.Buffered(1))


def _segment_sumsq(x, seg):
    sq = x * x
    hi = sq.astype(BF16)
    lo = (sq - hi.astype(F32)).astype(BF16)
    return (jnp.dot(hi, seg, preferred_element_type=F32)
            + jnp.dot(lo, seg, preferred_element_type=F32))


def _norm_rope(x, seg, gain, cos, sin_a, sin_b):
    width = x.shape[-1]
    ms = _segment_sumsq(x, seg) * (1.0 / HEAD_DIM)
    xn = x * lax.rsqrt(ms + RMS_EPS) * gain
    quarter = HEAD_DIM // 4
    up = pltpu.roll(xn, width - quarter, 1)
    down = pltpu.roll(xn, quarter, 1)
    return xn * cos + up * sin_a + down * sin_b


def _front_kernel(x_ref, w_ref, cos_ref, sina_ref, sinb_ref, qg_ref, kg_ref, seg_ref, wmix_ref,
                  pscale_ref, wdw_ref, bdw_ref, lng_ref, lnb_ref,
                  q_ref, k_ref, v_ref, pool_ref, conv_ref, ubuf, hbuf, hs_ref, cacc_ref):
    seq = x_ref.shape[1]
    rows = min(FRONT_ROWS, seq)
    o1, o2, o3 = ATTN_W, ATTN_W + KV_W, ATTN_W + 2 * KV_W
    o4, o5 = o3 + POOL_W, o3 + POOL_W + CONV_W

    ubuf[0:POOL_PAD, :] = jnp.zeros((POOL_PAD, POOL_W), F32)
    ubuf[POOL_PAD + seq:, :] = jnp.zeros((POOL_PAD, POOL_W), F32)
    hbuf[0:CONV_PAD, :] = jnp.zeros((CONV_PAD, CONV_W), F32)
    hbuf[CONV_PAD + seq:, :] = jnp.zeros((CONV_PAD, CONV_W), F32)

    def proj_chunk(c, carry):
        r0 = pl.multiple_of(c * rows, rows)
        xb = x_ref[0, pl.ds(r0, rows), :].astype(BF16)
        proj = jnp.dot(xb, w_ref[...], preferred_element_type=F32)
        cos = cos_ref[pl.ds(r0, rows), :]
        sin_a = sina_ref[pl.ds(r0, rows), :]
        sin_b = sinb_ref[pl.ds(r0, rows), :]
        rep = ATTN_W // LANES
        q = _norm_rope(proj[:, :o1], seg_ref[...], qg_ref[...],
                       jnp.concatenate([cos] * rep, axis=1),
                       jnp.concatenate([sin_a] * rep, axis=1),
                       jnp.concatenate([sin_b] * rep, axis=1))
        q_ref[0, :, pl.ds(r0, rows)] = (q * (HEAD_DIM ** -0.5 * LOG2_E)).T.astype(BF16)
        k = _norm_rope(proj[:, o1:o2], seg_ref[0:KV_W, 0:KV_W], kg_ref[...], cos, sin_a, sin_b)
        k_ref[0, pl.ds(r0, rows), :] = k.astype(BF16)
        v_ref[0, :, pl.ds(r0, rows)] = proj[:, o2:o3].T.astype(BF16)
        ubuf[pl.ds(POOL_PAD + r0, rows), :] = proj[:, o3:o4]
        hbuf[pl.ds(CONV_PAD + r0, rows), :] = proj[:, o4:o5] * _sigmoid(proj[:, o5:])
        return carry

    lax.fori_loop(0, seq // rows, proj_chunk, 0)

    lane = lax.broadcasted_iota(jnp.int32, (rows, LANES), 1)
    first = lane < POOL_GROUP
    for c in range(seq // rows):
        r0 = c * rows
        t = r0 + lax.broadcasted_iota(jnp.int32, (rows, LANES), 0)
        halves = []
        for half in range(POOL_W // LANES):
            w_a, w_b = POOL_WINDOWS[2 * half], POOL_WINDOWS[2 * half + 1]
            lo_a, hi_a = w_a // 2, w_a - w_a // 2 - 1
            lo_b, hi_b = w_b // 2, w_b - w_b // 2 - 1
            cols = slice(half * LANES, (half + 1) * LANES)

            def shifted(off):
                return ubuf[POOL_PAD + r0 + off:POOL_PAD + r0 + off + rows, cols]

            centre = shifted(0)
            tot_a = centre
            for off in range(-lo_a, hi_a + 1):
                if off != 0:
                    tot_a = tot_a + shifted(off)
            tot_b = tot_a
            for off in range(-lo_b, hi_b + 1):
                if off < -lo_a or off > hi_a:
                    tot_b = tot_b + shifted(off)
            lo = jnp.where(first, lo_a, lo_b)
            hi = jnp.where(first, hi_a, hi_b)
            cnt = (jnp.minimum(t + hi + 1, seq) - jnp.maximum(t - lo, 0)).astype(F32)
            tot = jnp.where(first, tot_a, tot_b)
            halves.append(tot / cnt - centre)
        pooled = jnp.concatenate(halves, axis=1).astype(BF16)
        mixed = jnp.dot(pooled, wmix_ref[...], preferred_element_type=F32) * pscale_ref[...]
        pool_ref[0, r0:r0 + rows, :] = mixed.astype(BF16)

    crow = min(CONV_ROWS, seq)
    shift0 = CONV_PAD - CONV_K // 2
    tail_rows = 2 * CONV_PAD - SUBLANES
    n_half = CONV_W // LANES
    for half in range(n_half):
        cols = slice(half * LANES, (half + 1) * LANES)

        def shift_chunk(c, carry):
            j0 = pl.multiple_of(c * crow, crow)
            win = hbuf[pl.ds(j0, crow + SUBLANES), cols]
            for s in range(1, SUBLANES):
                hs_ref[s - 1, pl.ds(j0, crow), :] = win[s:s + crow, :]
            return carry

        lax.fori_loop(0, seq // crow, shift_chunk, 0)
        tail = hbuf[seq:seq + 2 * CONV_PAD, cols]
        for s in range(1, SUBLANES):
            hs_ref[s - 1, seq:seq + tail_rows, :] = tail[s:s + tail_rows, :]

        def tap_chunk(c, carry):
            r0 = pl.multiple_of(c * crow, crow)
            acc = jnp.broadcast_to(bdw_ref[:, cols], (crow, LANES))
            for kk in range(CONV_K):
                a, s = divmod(kk + shift0, SUBLANES)
                start = r0 + SUBLANES * a
                src = hbuf[pl.ds(start, crow), cols] if s == 0 else hs_ref[s - 1, pl.ds(start, crow), :]
                acc = acc + src * wdw_ref[kk:kk + 1, cols]
            cacc_ref[pl.ds(r0, crow), cols] = acc
            return carry

        lax.fori_loop(0, seq // crow, tap_chunk, 0)

    def norm_chunk(c, carry):
        r0 = pl.multiple_of(c * rows, rows)
        hn = _layer_norm(cacc_ref[pl.ds(r0, rows), :], lng_ref[...], lnb_ref[...])
        conv_ref[0, pl.ds(r0, rows), :] = (hn * _sigmoid(hn)).astype(BF16)
        return carry

    lax.fori_loop(0, seq // rows, norm_chunk, 0)


def _front_call(x, fw):
    batch, seq, _ = x.shape
    seq_spec = lambda w: pl.BlockSpec((1, seq, w), lambda b: (b, 0, 0))
    feat_spec = lambda w: pl.BlockSpec((1, w, seq), lambda b: (b, 0, 0))
    out_shape = (
        jax.ShapeDtypeStruct((batch, ATTN_W, seq), BF16),
        jax.ShapeDtypeStruct((batch, seq, KV_W), BF16),
        jax.ShapeDtypeStruct((batch, KV_W, seq), BF16),
        jax.ShapeDtypeStruct((batch, seq, POOL_W), BF16),
        jax.ShapeDtypeStruct((batch, seq, CONV_W), BF16),
    )
    consts = (fw["w_front"], fw["cos"], fw["sin_a"], fw["sin_b"], fw["q_gain"], fw["k_gain"],
              fw["seg"], fw["w_mix"], fw["pool_scale"], fw["w_dw"], fw["b_dw"], fw["conv_ln_g"],
              fw["conv_ln_b"])
    return pl.pallas_call(
        _front_kernel,
        grid=(batch,),
        in_specs=[seq_spec(D_MODEL)] + [_const_spec(c.shape) for c in consts],
        out_specs=(feat_spec(ATTN_W), seq_spec(KV_W), feat_spec(KV_W), seq_spec(POOL_W),
                   seq_spec(CONV_W)),
        out_shape=out_shape,
        scratch_shapes=[pltpu.VMEM((seq + 2 * POOL_PAD, POOL_W), F32),
                        pltpu.VMEM((seq + 2 * CONV_PAD, CONV_W), F32),
                        pltpu.VMEM((SUBLANES - 1, seq + 2 * CONV_PAD, LANES), F32),
                        pltpu.VMEM((seq, CONV_W), F32)],
        compiler_params=pltpu.CompilerParams(dimension_semantics=("arbitrary",),
                                             vmem_limit_bytes=VMEM_LIMIT),
        name="front",
    )(x, *consts)


def _attn_kernel(qt_ref, k_ref, vt_ref, o_ref, st_even_ref, st_odd_ref):
    st_refs = (st_even_ref, st_odd_ref)
    seq = k_ref.shape[1]
    kt = min(ATTN_KT, seq)
    n_kt = seq // kt
    group = N_HEADS // N_KV_HEADS

    def kv_rows(head):
        kv = head // group
        return slice(kv * HEAD_DIM, (kv + 1) * HEAD_DIM)

    def score_tile(head, j, slot, m_run):
        qt = qt_ref[0, head * HEAD_DIM:(head + 1) * HEAD_DIM, :]
        st = jnp.dot(k_ref[0, j * kt:(j + 1) * kt, kv_rows(head)], qt,
                     preferred_element_type=F32)
        st_refs[slot][j * kt:(j + 1) * kt, :] = st
        m_tile = jnp.max(st, axis=0, keepdims=True)
        return m_tile if m_run is None else jnp.maximum(m_run, m_tile)

    def value_tile(head, j, pt, ot):
        vt_ones = jnp.concatenate(
            [vt_ref[0, kv_rows(head), j * kt:(j + 1) * kt], jnp.ones((ONES_ROWS, kt), BF16)],
            axis=0)
        part = jnp.dot(vt_ones, pt, preferred_element_type=F32)
        return part if ot is None else ot + part

    m_next = None
    for j in range(n_kt):
        m_next = score_tile(0, j, 0, m_next)
    outs = []
    for head in range(N_HEADS):
        slot = head % 2
        m, m_next = m_next, None
        ot = None
        if head + 1 < N_HEADS:
            for j in range(n_kt):
                m_next = score_tile(head + 1, j, 1 - slot, m_next)
        for j in range(n_kt):
            pt = jnp.exp2(st_refs[slot][j * kt:(j + 1) * kt, :] - m).astype(BF16)
            ot = value_tile(head, j, pt, ot)
        outs.append(ot[:HEAD_DIM, :] / ot[HEAD_DIM:HEAD_DIM + 1, :])
    o_ref[0] = jnp.concatenate(outs, axis=0).T.astype(BF16)


def _attn_call(qt, k, vt):
    batch, seq, _ = k.shape
    tq = min(ATTN_TQ, seq)
    return pl.pallas_call(
        _attn_kernel,
        grid=(batch, seq // tq),
        in_specs=[pl.BlockSpec((1, ATTN_W, tq), lambda b, i: (b, 0, i)),
                  pl.BlockSpec((1, seq, KV_W), lambda b, i: (b, 0, 0)),
                  pl.BlockSpec((1, KV_W, seq), lambda b, i: (b, 0, 0))],
        out_specs=pl.BlockSpec((1, tq, ATTN_W), lambda b, i: (b, i, 0)),
        out_shape=jax.ShapeDtypeStruct((batch, seq, ATTN_W), BF16),
        scratch_shapes=[pltpu.VMEM((seq, tq), F32), pltpu.VMEM((seq, tq), F32)],
        compiler_params=pltpu.CompilerParams(dimension_semantics=("arbitrary", "arbitrary"),
                                             vmem_limit_bytes=VMEM_LIMIT),
        name="attention",
    )(qt, k, vt)


def _merge_kernel(x_ref, a_ref, p_ref, c_ref, wg_ref, bg_ref, wa_ref, wp_ref, wc_ref, wo_ref,
                  g_ref, b_ref, o_ref):
    x = x_ref[...]
    xb = x.astype(BF16)
    merged = None
    for br, (in_ref, w_ref) in enumerate(((a_ref, wa_ref), (p_ref, wp_ref), (c_ref, wc_ref))):
        cols = slice(br * D_MODEL, (br + 1) * D_MODEL)
        gate = _sigmoid(jnp.dot(xb, wg_ref[:, cols], preferred_element_type=F32) + bg_ref[:, cols])
        term = gate * jnp.dot(in_ref[...], w_ref[...], preferred_element_type=F32)
        merged = term if merged is None else merged + term
    mix = jnp.dot(merged.astype(BF16), wo_ref[...], preferred_element_type=F32)
    o_ref[...] = _layer_norm(ALPHA * x + mix, g_ref[...], b_ref[...])


def _merge_call(x, attn, pool, conv, mw):
    n = x.shape[0]
    tm = min(TOKEN_TILE, n)
    row_spec = lambda w: pl.BlockSpec((tm, w), lambda i: (i, 0))
    consts = (mw["w_gate"], mw["b_gate"], mw["w_attn_out"], mw["w_pool_out"], mw["w_conv_out"],
              mw["w_out"], mw["ln1_g"], mw["ln1_b"])
    return pl.pallas_call(
        _merge_kernel,
        grid=(n // tm,),
        in_specs=[row_spec(D_MODEL), row_spec(ATTN_W), row_spec(POOL_W), row_spec(CONV_W)]
        + [_const_spec(c.shape) for c in consts],
        out_specs=row_spec(D_MODEL),
        out_shape=jax.ShapeDtypeStruct((n, D_MODEL), F32),
        compiler_params=pltpu.CompilerParams(dimension_semantics=("arbitrary",),
                                             vmem_limit_bytes=VMEM_LIMIT),
        name="merge",
    )(x, attn, pool, conv, *consts)


def _swiglu_partial(xb, wg, wu, wd):
    hg = jnp.dot(xb, wg, preferred_element_type=F32)
    hu = jnp.dot(xb, wu, preferred_element_type=F32)
    h = (hg * _sigmoid(hg) * hu).astype(BF16)
    return jnp.dot(h, wd, preferred_element_type=F32)


def _ffn_kernel(x_ref, wg_ref, wu_ref, wd_ref, g_ref, b_ref, o_ref):
    x = x_ref[...]
    xb = x.astype(BF16)
    half = D_FF // 2
    f = None
    for c in range(2):
        cols = slice(c * half, (c + 1) * half)
        part = _swiglu_partial(xb, wg_ref[:, cols], wu_ref[:, cols], wd_ref[cols, :])
        f = part if f is None else f + part
    o_ref[...] = _layer_norm(ALPHA * x + f, g_ref[...], b_ref[...])


def _ffn_call(x, dw):
    n = x.shape[0]
    tm = min(TOKEN_TILE, n)
    row_spec = pl.BlockSpec((tm, D_MODEL), lambda i: (i, 0))
    consts = (dw["w_gate"], dw["w_up"], dw["w_down"], dw["ln2_g"], dw["ln2_b"])
    return pl.pallas_call(
        _ffn_kernel,
        grid=(n // tm,),
        in_specs=[row_spec] + [_const_spec(c.shape) for c in consts],
        out_specs=row_spec,
        out_shape=jax.ShapeDtypeStruct((n, D_MODEL), F32),
        compiler_params=pltpu.CompilerParams(dimension_semantics=("arbitrary",),
                                             vmem_limit_bytes=VMEM_LIMIT),
        name="ffn",
    )(x, *consts)


def _route(logits, lane):
    m1 = jnp.max(logits, axis=-1, keepdims=True)
    i1 = jnp.min(jnp.where(logits == m1, lane, LANES), axis=-1, keepdims=True)
    rest = jnp.where(lane == i1, NEG_BIG, logits)
    m2 = jnp.max(rest, axis=-1, keepdims=True)
    i2 = jnp.min(jnp.where(rest == m2, lane, LANES), axis=-1, keepdims=True)
    e2 = jnp.exp(m2 - m1)
    w1 = 1.0 / (1.0 + e2)
    w2 = e2 / (1.0 + e2)
    mask = jnp.where(lane == i1, 1.0, jnp.where(lane == i2, 1.0, 0.0))
    comb = jnp.where(lane == i1, w1, jnp.where(lane == i2, w2, 0.0))
    return mask, comb


def _moe_kernel(x_ref, wr_ref, br_ref, tri_ref, wg_ref, wu_ref, wd_ref, g_ref, b_ref, o_ref,
                comb_ref, rank_ref, rankt_ref, cnt_ref, offs_ref, xy_ref):
    tile = x_ref.shape[0]
    n_sub = tile // MOE_SUB
    e = pl.program_id(1)
    lane = lax.broadcasted_iota(jnp.int32, (MOE_SUB, LANES), 1)
    slot_rows = lax.broadcasted_iota(jnp.int32, (MOE_SLOTS, MOE_SUB), 0).astype(F32)
    slot_cols = lax.broadcasted_iota(jnp.int32, (MOE_SUB, MOE_SUB), 1).astype(F32)

    @pl.when(jnp.logical_and(pl.program_id(0) == 0, e == 0))
    def _():
        xy_ref[...] = jnp.zeros_like(xy_ref)

    @pl.when(e == 0)
    def _():
        for sb in range(n_sub):
            rows = slice(sb * MOE_SUB, (sb + 1) * MOE_SUB)
            xb = x_ref[rows, :].astype(BF16)
            logits = jnp.dot(xb, wr_ref[...], preferred_element_type=F32) + br_ref[...]
            mask, comb = _route(logits, lane)
            comb_ref[rows, :] = comb
            rank = jnp.dot(tri_ref[...], mask.astype(BF16), preferred_element_type=F32)
            rank = jnp.where(mask > 0.0, rank, -1.0)
            rank_ref[rows, :] = rank
            rankt_ref[:, rows] = rank.T
            counts = jnp.sum(mask, axis=0, keepdims=True).astype(jnp.int32)
            for ee in range(N_EXPERTS):
                cnt_ref[sb * N_EXPERTS + ee] = counts[0, ee]
        o_ref[...] = jnp.zeros_like(o_ref)

    def gather(sb, first_slot):
        rows = slice(sb * MOE_SUB, (sb + 1) * MOE_SUB)
        slot_of_token = rankt_ref[pl.ds(e, 1), rows] - first_slot
        take = jnp.where(slot_rows == slot_of_token, 1.0, 0.0).astype(BF16)
        return jnp.dot(take, x_ref[rows, :].astype(BF16), preferred_element_type=F32)

    off = 0
    for sb in range(n_sub):
        xy_ref[pl.ds(pl.multiple_of(off, MOE_ALIGN), MOE_SLOTS), :] = gather(sb, 0.0).astype(BF16)
        offs_ref[sb] = off
        count = cnt_ref[sb * N_EXPERTS + e]
        off = off + ((count + (MOE_ALIGN - 1)) // MOE_ALIGN) * MOE_ALIGN
    n_rows = off

    for sb in range(n_sub):
        count = cnt_ref[sb * N_EXPERTS + e]

        @pl.when(count > MOE_SLOTS)
        def _():
            start = pl.multiple_of(offs_ref[sb] + MOE_SLOTS, MOE_ALIGN)
            row = lax.broadcasted_iota(jnp.int32, (MOE_SLOTS, D_MODEL), 0)
            rest = gather(sb, float(MOE_SLOTS))
            keep = xy_ref[pl.ds(start, MOE_SLOTS), :].astype(F32)
            merged = jnp.where(row < count - MOE_SLOTS, rest, keep)
            xy_ref[pl.ds(start, MOE_SLOTS), :] = merged.astype(BF16)

    def expert_rows(start, size):
        xc = xy_ref[pl.ds(start, size), :]
        y = _swiglu_partial(xc, wg_ref[0], wu_ref[0], wd_ref[0])
        xy_ref[pl.ds(start, size), :] = y.astype(BF16)

    n_full = n_rows // MOE_CHUNK

    def full_body(i, carry):
        expert_rows(pl.multiple_of(i * MOE_CHUNK, MOE_CHUNK), MOE_CHUNK)
        return carry

    lax.fori_loop(0, n_full, full_body, 0)
    n_tail = (n_rows - n_full * MOE_CHUNK + (MOE_TAIL - 1)) // MOE_TAIL

    def tail_body(j, carry):
        expert_rows(pl.multiple_of(n_full * MOE_CHUNK + j * MOE_TAIL, MOE_TAIL), MOE_TAIL)
        return carry

    lax.fori_loop(0, n_tail, tail_body, 0)

    for sb in range(n_sub):
        rows = slice(sb * MOE_SUB, (sb + 1) * MOE_SUB)
        here = lane == e
        slot = jnp.sum(jnp.where(here, rank_ref[rows, :], 0.0), axis=-1, keepdims=True)
        weight = jnp.sum(jnp.where(here, comb_ref[rows, :], 0.0), axis=-1, keepdims=True)
        put = jnp.where(slot_cols == slot, 1.0, 0.0).astype(BF16)
        yb = xy_ref[pl.ds(pl.multiple_of(offs_ref[sb], MOE_ALIGN), MOE_SUB), :]
        o_ref[rows, :] += weight * jnp.dot(put, yb, preferred_element_type=F32)

    @pl.when(e == N_EXPERTS - 1)
    def _():
        o_ref[...] = _layer_norm(ALPHA * x_ref[...] + o_ref[...], g_ref[...], b_ref[...])


def _moe_call(x, ew):
    n = x.shape[0]
    tile = min(MOE_TILE, n)
    n_sub = tile // MOE_SUB
    row_spec = lambda **kw: pl.BlockSpec((tile, D_MODEL), lambda i, e: (i, 0), **kw)
    up_spec = pl.BlockSpec((1, D_MODEL, D_FF_EXPERT), lambda i, e: (e, 0, 0))
    down_spec = pl.BlockSpec((1, D_FF_EXPERT, D_MODEL), lambda i, e: (e, 0, 0))
    consts = (ew["w_router"], ew["b_router"], ew["tri"])
    small = (ew["ln2_g"], ew["ln2_b"])
    return pl.pallas_call(
        _moe_kernel,
        grid=(n // tile, N_EXPERTS),
        in_specs=[row_spec(pipeline_mode=pl.Buffered(1))] + [_const_spec(c.shape) for c in consts]
        + [up_spec, up_spec, down_spec] + [_const_spec(c.shape) for c in small],
        out_specs=row_spec(),
        out_shape=jax.ShapeDtypeStruct((n, D_MODEL), F32),
        scratch_shapes=[pltpu.VMEM((tile, LANES), F32), pltpu.VMEM((tile, LANES), F32),
                        pltpu.VMEM((LANES, tile), F32),
                        pltpu.SMEM((n_sub * N_EXPERTS,), jnp.int32),
                        pltpu.SMEM((n_sub,), jnp.int32),
                        pltpu.VMEM((tile + 2 * MOE_SUB, D_MODEL), BF16)],
        compiler_params=pltpu.CompilerParams(dimension_semantics=("arbitrary", "arbitrary"),
                                             vmem_limit_bytes=VMEM_LIMIT_HIGH),
        name="experts",
    )(x, *consts, ew["w_gate"], ew["w_up"], ew["w_down"], *small)


def _rope_tables(seq):
    rows = seq // GRID_W
    row = jnp.repeat(jnp.arange(rows), GRID_W).astype(F32)
    col = jnp.tile(jnp.arange(GRID_W), rows).astype(F32)
    n_freq = HEAD_DIM // 4
    inv = ROPE_THETA ** (-jnp.arange(n_freq, dtype=F32) / n_freq)
    ang = jnp.concatenate([row[:, None] * inv] * 2 + [col[:, None] * inv] * 2, axis=1)
    ang = jnp.concatenate([ang] * (LANES // HEAD_DIM), axis=1)
    first = (jnp.arange(LANES) % (2 * n_freq)) < n_freq
    cos, sin = jnp.cos(ang), jnp.sin(ang)
    return cos, jnp.where(first, -sin, 0.0), jnp.where(first, 0.0, sin)


def _row(v):
    return v.reshape(1, -1).astype(F32)


def _prepare(p, seq):
    cos, sin_a, sin_b = _rope_tables(seq)
    head = jnp.arange(ATTN_W) // HEAD_DIM
    seg = (head[:, None] == head[None, :]).astype(BF16)
    layers = []
    for l in range(DEPTH):
        w_in = p["w_in"][l]
        front = dict(
            w_front=w_in[:, :FRONT_W].astype(BF16), cos=cos, sin_a=sin_a, sin_b=sin_b,
            q_gain=_row(jnp.tile(p["q_norm_g"][l], N_HEADS)),
            k_gain=_row(jnp.tile(p["k_norm_g"][l], N_KV_HEADS)),
            seg=seg,
            w_mix=jax.scipy.linalg.block_diag(*p["w_pool_mix"][l]).astype(BF16),
            pool_scale=_row(p["pool_scale"][l]), w_dw=p["w_dw"][l].astype(F32),
            b_dw=_row(p["b_dw"][l]), conv_ln_g=_row(p["conv_ln_g"][l]),
            conv_ln_b=_row(p["conv_ln_b"][l]))
        merge = dict(
            w_gate=w_in[:, FRONT_W:].astype(BF16), b_gate=_row(p["b_gate"][l]),
            w_attn_out=p["w_attn_out"][l].astype(BF16), w_pool_out=p["w_pool_out"][l].astype(BF16),
            w_conv_out=p["w_conv_out"][l].astype(BF16), w_out=p["w_out"][l].astype(BF16),
            ln1_g=_row(p["ln1_g"][l]), ln1_b=_row(p["ln1_b"][l]))
        j = l // 2
        if l % 2 == 0:
            mixer = dict(w_gate=p["w_ff_gate"][j].astype(BF16), w_up=p["w_ff_up"][j].astype(BF16),
                         w_down=p["w_ff_down"][j].astype(BF16))
        else:
            pad = LANES - N_EXPERTS
            sub = jnp.arange(MOE_SUB)
            mixer = dict(
                tri=(sub[None, :] < sub[:, None]).astype(BF16),
                w_router=jnp.pad(p["w_router"][j], ((0, 0), (0, pad))).astype(BF16),
                b_router=jnp.pad(_row(p["b_router"][j]), ((0, 0), (0, pad)),
                                 constant_values=NEG_BIG),
                w_gate=p["w_e_gate"][j].astype(BF16), w_up=p["w_e_up"][j].astype(BF16),
                w_down=p["w_e_down"][j].astype(BF16))
        mixer["ln2_g"] = _row(p["ln2_g"][l])
        mixer["ln2_b"] = _row(p["ln2_b"][l])
        layers.append((front, merge, mixer))
    return layers


def _trunk(x, layers):
    batch, seq, _ = x.shape
    for l, (front, merge, mixer) in enumerate(layers):
        q, k, v, pool, conv = _front_call(x, front)
        attn = _attn_call(q, k, v)
        flat = lambda a: a.reshape(batch * seq, a.shape[-1])
        x1 = _merge_call(flat(x), flat(attn), flat(pool), flat(conv), merge)
        x2 = _ffn_call(x1, mixer) if l % 2 == 0 else _moe_call(x1, mixer)
        x = x2.reshape(batch, seq, D_MODEL)
    return x


def kernel(x_prompt, x_sample, w_in, b_gate, q_norm_g, k_norm_g, w_attn_out, w_pool_mix, pool_scale,
           w_pool_out, w_dw, b_dw, conv_ln_g, conv_ln_b, w_conv_out, w_out, ln1_g, ln1_b,
           w_ff_gate, w_ff_up, w_ff_down, w_router, b_router, w_e_gate, w_e_up, w_e_down,
           ln2_g, ln2_b):
    params = dict(w_in=w_in, b_gate=b_gate, q_norm_g=q_norm_g, k_norm_g=k_norm_g,
                  w_attn_out=w_attn_out, w_pool_mix=w_pool_mix, pool_scale=pool_scale,
                  w_pool_out=w_pool_out, w_dw=w_dw, b_dw=b_dw, conv_ln_g=conv_ln_g,
                  conv_ln_b=conv_ln_b, w_conv_out=w_conv_out, w_out=w_out, ln1_g=ln1_g, ln1_b=ln1_b,
                  w_ff_gate=w_ff_gate, w_ff_up=w_ff_up, w_ff_down=w_ff_down, w_router=w_router,
                  b_router=b_router, w_e_gate=w_e_gate, w_e_up=w_e_up, w_e_down=w_e_down,
                  ln2_g=ln2_g, ln2_b=ln2_b)
    assert x_prompt.shape[1] == x_sample.shape[1]
    layers = _prepare(params, x_prompt.shape[1])
    return _trunk(x_prompt, layers), _trunk(x_sample, layers)
```

```python
import functools

import jax
import jax.numpy as jnp
from jax import lax
from jax.experimental import pallas as pl
from jax.experimental.pallas import tpu as pltpu

D_MODEL = 1024
DEPTH = 2
GRID_W = 64
N_HEADS = 8
N_KV_HEADS = 2
HEAD_DIM = 64
ATTN_W = N_HEADS * HEAD_DIM
KV_W = N_KV_HEADS * HEAD_DIM
ROPE_THETA = 10000.0
POOL_WINDOWS = (2, 4, 8, 16)
POOL_GROUP = 64
POOL_W = POOL_GROUP * len(POOL_WINDOWS)
CONV_W = 256
CONV_K = 31
N_BRANCH = 3
FRONT_W = ATTN_W + 2 * KV_W + POOL_W + 2 * CONV_W
D_FF = 2816
N_EXPERTS = 8
D_FF_EXPERT = 1408
ALPHA = (2 * DEPTH) ** 0.25
LN_EPS = 1e-5
RMS_EPS = 1e-6

LANES = 128
SUBLANES = 8
POOL_PAD = 8
CONV_PAD = 16
FRONT_ROWS = 256
CONV_ROWS = 64
ATTN_TQ = 512
LOG2_E = 1.4426950408889634
TOKEN_TILE = 512
MOE_TILE = 2048
MOE_SUB = 256
MOE_ALIGN = 16
MOE_SLOTS = MOE_SUB // 2
MOE_CHUNK = 256
MOE_TAIL = 128
NEG_BIG = -1e30
VMEM_LIMIT = 56 * 1024 * 1024
VMEM_LIMIT_HIGH = 60 * 1024 * 1024

F32 = jnp.float32
BF16 = jnp.bfloat16


def _sigmoid(x):
    return 1.0 / (1.0 + jnp.exp(-x))


def _layer_norm(z, g, b):
    mu = jnp.mean(z, axis=-1, keepdims=True)
    d = z - mu
    var = jnp.mean(d * d, axis=-1, keepdims=True)
    return d * lax.rsqrt(var + LN_EPS) * g + b


def _const_spec(shape):
    zeros = (0,) * len(shape)
    return pl.BlockSpec(shape, lambda *_: zeros, pipeline_mode=pl.Buffered(1))


def _segment_sumsq(x, seg):
    sq = x * x
    hi = sq.astype(BF16)
    lo = (sq - hi.astype(F32)).astype(BF16)
    return (jnp.dot(hi, seg, preferred_element_type=F32)
            + jnp.dot(lo, seg, preferred_element_type=F32))


def _norm_rope(x, seg, gain, cos, sin_a, sin_b):
    width = x.shape[-1]
    ms = _segment_sumsq(x, seg) * (1.0 / HEAD_DIM)
    xn = x * lax.rsqrt(ms + RMS_EPS) * gain
    quarter = HEAD_DIM // 4
    up = pltpu.roll(xn, width - quarter, 1)
    down = pltpu.roll(xn, quarter, 1)
    return xn * cos + up * sin_a + down * sin_b


def _front_kernel(x_ref, w_ref, cos_ref, sina_ref, sinb_ref, qg_ref, kg_ref, seg_ref, wmix_ref,
                  pscale_ref, wdw_ref, bdw_ref, lng_ref, lnb_ref,
                  q_ref, k_ref, v_ref, pool_ref, conv_ref, ubuf, hbuf, hs_ref, cacc_ref):
    seq = x_ref.shape[1]
    rows = min(FRONT_ROWS, seq)
    o1, o2, o3 = ATTN_W, ATTN_W + KV_W, ATTN_W + 2 * KV_W
    o4, o5 = o3 + POOL_W, o3 + POOL_W + CONV_W

    ubuf[0:POOL_PAD, :] = jnp.zeros((POOL_PAD, POOL_W), F32)
    ubuf[POOL_PAD + seq:, :] = jnp.zeros((POOL_PAD, POOL_W), F32)
    hbuf[0:CONV_PAD, :] = jnp.zeros((CONV_PAD, CONV_W), F32)
    hbuf[CONV_PAD + seq:, :] = jnp.zeros((CONV_PAD, CONV_W), F32)

    def proj_chunk(c, carry):
        r0 = pl.multiple_of(c * rows, rows)
        xb = x_ref[0, pl.ds(r0, rows), :].astype(BF16)
        proj = jnp.dot(xb, w_ref[...], preferred_element_type=F32)
        cos = cos_ref[pl.ds(r0, rows), :]
        sin_a = sina_ref[pl.ds(r0, rows), :]
        sin_b = sinb_ref[pl.ds(r0, rows), :]
        rep = ATTN_W // LANES
        q = _norm_rope(proj[:, :o1], seg_ref[...], qg_ref[...],
                       jnp.concatenate([cos] * rep, axis=1),
                       jnp.concatenate([sin_a] * rep, axis=1),
                       jnp.concatenate([sin_b] * rep, axis=1))
        q_ref[0, pl.ds(r0, rows), :] = (q * (HEAD_DIM ** -0.5 * LOG2_E)).astype(BF16)
        k = _norm_rope(proj[:, o1:o2], seg_ref[0:KV_W, 0:KV_W], kg_ref[...], cos, sin_a, sin_b)
        k_ref[0, pl.ds(r0, rows), :] = k.astype(BF16)
        v = proj[:, o2:o3]
        low = lax.broadcasted_iota(jnp.int32, v.shape, 1) < HEAD_DIM
        v_ones = [jnp.where(low, v, 1.0), jnp.where(low, pltpu.roll(v, HEAD_DIM, 1), 1.0)]
        v_ref[0, pl.ds(r0, rows), :] = jnp.concatenate(v_ones, axis=1).astype(BF16)
        ubuf[pl.ds(POOL_PAD + r0, rows), :] = proj[:, o3:o4]
        hbuf[pl.ds(CONV_PAD + r0, rows), :] = proj[:, o4:o5] * _sigmoid(proj[:, o5:])
        return carry

    lax.fori_loop(0, seq // rows, proj_chunk, 0)

    lane = lax.broadcasted_iota(jnp.int32, (rows, LANES), 1)
    first = lane < POOL_GROUP
    for c in range(seq // rows):
        r0 = c * rows
        t = r0 + lax.broadcasted_iota(jnp.int32, (rows, LANES), 0)
        halves = []
        for half in range(POOL_W // LANES):
            w_a, w_b = POOL_WINDOWS[2 * half], POOL_WINDOWS[2 * half + 1]
            lo_a, hi_a = w_a // 2, w_a - w_a // 2 - 1
            lo_b, hi_b = w_b // 2, w_b - w_b // 2 - 1
            cols = slice(half * LANES, (half + 1) * LANES)

            def shifted(off):
                return ubuf[POOL_PAD + r0 + off:POOL_PAD + r0 + off + rows, cols]

            centre = shifted(0)
            tot_a = centre
            for off in range(-lo_a, hi_a + 1):
                if off != 0:
                    tot_a = tot_a + shifted(off)
            tot_b = tot_a
            for off in range(-lo_b, hi_b + 1):
                if off < -lo_a or off > hi_a:
                    tot_b = tot_b + shifted(off)
            lo = jnp.where(first, lo_a, lo_b)
            hi = jnp.where(first, hi_a, hi_b)
            cnt = (jnp.minimum(t + hi + 1, seq) - jnp.maximum(t - lo, 0)).astype(F32)
            tot = jnp.where(first, tot_a, tot_b)
            halves.append(tot / cnt - centre)
        pooled = jnp.concatenate(halves, axis=1).astype(BF16)
        mixed = jnp.dot(pooled, wmix_ref[...], preferred_element_type=F32) * pscale_ref[...]
        pool_ref[0, r0:r0 + rows, :] = mixed.astype(BF16)

    crow = min(CONV_ROWS, seq)
    shift0 = CONV_PAD - CONV_K // 2
    tail_rows = 2 * CONV_PAD - SUBLANES
    n_half = CONV_W // LANES
    for half in range(n_half):
        cols = slice(half * LANES, (half + 1) * LANES)

        def shift_chunk(c, carry):
            j0 = pl.multiple_of(c * crow, crow)
            win = hbuf[pl.ds(j0, crow + SUBLANES), cols]
            for s in range(1, SUBLANES):
                hs_ref[s - 1, pl.ds(j0, crow), :] = win[s:s + crow, :]
            return carry

        lax.fori_loop(0, seq // crow, shift_chunk, 0)
        tail = hbuf[seq:seq + 2 * CONV_PAD, cols]
        for s in range(1, SUBLANES):
            hs_ref[s - 1, seq:seq + tail_rows, :] = tail[s:s + tail_rows, :]

        def tap_chunk(c, carry):
            r0 = pl.multiple_of(c * crow, crow)
            acc = jnp.broadcast_to(bdw_ref[:, cols], (crow, LANES))
            for kk in range(CONV_K):
                a, s = divmod(kk + shift0, SUBLANES)
                start = r0 + SUBLANES * a
                src = hbuf[pl.ds(start, crow), cols] if s == 0 else hs_ref[s - 1, pl.ds(start, crow), :]
                acc = acc + src * wdw_ref[kk:kk + 1, cols]
            cacc_ref[pl.ds(r0, crow), cols] = acc
            return carry

        lax.fori_loop(0, seq // crow, tap_chunk, 0)

    def norm_chunk(c, carry):
        r0 = pl.multiple_of(c * rows, rows)
        hn = _layer_norm(cacc_ref[pl.ds(r0, rows), :], lng_ref[...], lnb_ref[...])
        conv_ref[0, pl.ds(r0, rows), :] = (hn * _sigmoid(hn)).astype(BF16)
        return carry

    lax.fori_loop(0, seq // rows, norm_chunk, 0)


def _front_call(x, fw):
    batch, seq, _ = x.shape
    seq_spec = lambda w: pl.BlockSpec((1, seq, w), lambda b: (b, 0, 0))
    out_shape = (
        jax.ShapeDtypeStruct((batch, seq, ATTN_W), BF16),
        jax.ShapeDtypeStruct((batch, seq, KV_W), BF16),
        jax.ShapeDtypeStruct((batch, seq, 2 * KV_W), BF16),
        jax.ShapeDtypeStruct((batch, seq, POOL_W), BF16),
        jax.ShapeDtypeStruct((batch, seq, CONV_W), BF16),
    )
    consts = (fw["w_front"], fw["cos"], fw["sin_a"], fw["sin_b"], fw["q_gain"], fw["k_gain"],
              fw["seg"], fw["w_mix"], fw["pool_scale"], fw["w_dw"], fw["b_dw"], fw["conv_ln_g"],
              fw["conv_ln_b"])
    return pl.pallas_call(
        _front_kernel,
        grid=(batch,),
        in_specs=[seq_spec(D_MODEL)] + [_const_spec(c.shape) for c in consts],
        out_specs=(seq_spec(ATTN_W), seq_spec(KV_W), seq_spec(2 * KV_W), seq_spec(POOL_W),
                   seq_spec(CONV_W)),
        out_shape=out_shape,
        scratch_shapes=[pltpu.VMEM((seq + 2 * POOL_PAD, POOL_W), F32),
                        pltpu.VMEM((seq + 2 * CONV_PAD, CONV_W), F32),
                        pltpu.VMEM((SUBLANES - 1, seq + 2 * CONV_PAD, LANES), F32),
                        pltpu.VMEM((seq, CONV_W), F32)],
        compiler_params=pltpu.CompilerParams(dimension_semantics=("arbitrary",),
                                             vmem_limit_bytes=VMEM_LIMIT),
        name="front",
    )(x, *consts)


def _attn_kernel(q_ref, k_ref, v_ref, o_ref):
    group = N_HEADS // N_KV_HEADS
    for kv in range(N_KV_HEADS):
        k = k_ref[0, :, kv * HEAD_DIM:(kv + 1) * HEAD_DIM]
        v_ones = v_ref[0, :, 2 * kv * HEAD_DIM:2 * (kv + 1) * HEAD_DIM]
        for g in range(group):
            hcols = slice((kv * group + g) * HEAD_DIM, (kv * group + g + 1) * HEAD_DIM)
            q = q_ref[0, :, hcols]
            s = lax.dot_general(q, k, (((1,), (1,)), ((), ())), preferred_element_type=F32)
            m = jnp.max(s, axis=-1, keepdims=True)
            p = jnp.exp2(s - m).astype(BF16)
            o = jnp.dot(p, v_ones, preferred_element_type=F32)
            o = o / pltpu.roll(o, HEAD_DIM, 1)
            o_ref[0, :, hcols] = o[:, :HEAD_DIM].astype(BF16)


def _attn_call(q, k, v_ones):
    batch, seq, _ = q.shape
    tq = min(ATTN_TQ, seq)
    return pl.pallas_call(
        _attn_kernel,
        grid=(batch, seq // tq),
        in_specs=[pl.BlockSpec((1, tq, ATTN_W), lambda b, i: (b, i, 0)),
                  pl.BlockSpec((1, seq, KV_W), lambda b, i: (b, 0, 0)),
                  pl.BlockSpec((1, seq, 2 * KV_W), lambda b, i: (b, 0, 0))],
        out_specs=pl.BlockSpec((1, tq, ATTN_W), lambda b, i: (b, i, 0)),
        out_shape=jax.ShapeDtypeStruct((batch, seq, ATTN_W), BF16),
        compiler_params=pltpu.CompilerParams(dimension_semantics=("arbitrary", "arbitrary"),
                                             vmem_limit_bytes=VMEM_LIMIT),
        name="attention",
    )(q, k, v_ones)


def _merge_kernel(x_ref, a_ref, p_ref, c_ref, wg_ref, bg_ref, wa_ref, wp_ref, wc_ref, wo_ref,
                  g_ref, b_ref, o_ref):
    x = x_ref[...]
    xb = x.astype(BF16)
    merged = None
    for br, (in_ref, w_ref) in enumerate(((a_ref, wa_ref), (p_ref, wp_ref), (c_ref, wc_ref))):
        cols = slice(br * D_MODEL, (br + 1) * D_MODEL)
        gate = _sigmoid(jnp.dot(xb, wg_ref[:, cols], preferred_element_type=F32) + bg_ref[:, cols])
        term = gate * jnp.dot(in_ref[...], w_ref[...], preferred_element_type=F32)
        merged = term if merged is None else merged + term
    mix = jnp.dot(merged.astype(BF16), wo_ref[...], preferred_element_type=F32)
    o_ref[...] = _layer_norm(ALPHA * x + mix, g_ref[...], b_ref[...])


def _merge_call(x, attn, pool, conv, mw):
    n = x.shape[0]
    tm = min(TOKEN_TILE, n)
    row_spec = lambda w: pl.BlockSpec((tm, w), lambda i: (i, 0))
    consts = (mw["w_gate"], mw["b_gate"], mw["w_attn_out"], mw["w_pool_out"], mw["w_conv_out"],
              mw["w_out"], mw["ln1_g"], mw["ln1_b"])
    return pl.pallas_call(
        _merge_kernel,
        grid=(n // tm,),
        in_specs=[row_spec(D_MODEL), row_spec(ATTN_W), row_spec(POOL_W), row_spec(CONV_W)]
        + [_const_spec(c.shape) for c in consts],
        out_specs=row_spec(D_MODEL),
        out_shape=jax.ShapeDtypeStruct((n, D_MODEL), F32),
        compiler_params=pltpu.CompilerParams(dimension_semantics=("arbitrary",),
                                             vmem_limit_bytes=VMEM_LIMIT),
        name="merge",
    )(x, attn, pool, conv, *consts)


def _swiglu_partial(xb, wg, wu, wd):
    hg = jnp.dot(xb, wg, preferred_element_type=F32)
    hu = jnp.dot(xb, wu, preferred_element_type=F32)
    h = (hg * _sigmoid(hg) * hu).astype(BF16)
    return jnp.dot(h, wd, preferred_element_type=F32)


def _ffn_kernel(x_ref, wg_ref, wu_ref, wd_ref, g_ref, b_ref, o_ref):
    x = x_ref[...]
    xb = x.astype(BF16)
    half = D_FF // 2
    f = None
    for c in range(2):
        cols = slice(c * half, (c + 1) * half)
        part = _swiglu_partial(xb, wg_ref[:, cols], wu_ref[:, cols], wd_ref[cols, :])
        f = part if f is None else f + part
    o_ref[...] = _layer_norm(ALPHA * x + f, g_ref[...], b_ref[...])


def _ffn_call(x, dw):
    n = x.shape[0]
    tm = min(TOKEN_TILE, n)
    row_spec = pl.BlockSpec((tm, D_MODEL), lambda i: (i, 0))
    consts = (dw["w_gate"], dw["w_up"], dw["w_down"], dw["ln2_g"], dw["ln2_b"])
    return pl.pallas_call(
        _ffn_kernel,
        grid=(n // tm,),
        in_specs=[row_spec] + [_const_spec(c.shape) for c in consts],
        out_specs=row_spec,
        out_shape=jax.ShapeDtypeStruct((n, D_MODEL), F32),
        compiler_params=pltpu.CompilerParams(dimension_semantics=("arbitrary",),
                                             vmem_limit_bytes=VMEM_LIMIT),
        name="ffn",
    )(x, *consts)


def _route(logits, lane):
    m1 = jnp.max(logits, axis=-1, keepdims=True)
    i1 = jnp.min(jnp.where(logits == m1, lane, LANES), axis=-1, keepdims=True)
    rest = jnp.where(lane == i1, NEG_BIG, logits)
    m2 = jnp.max(rest, axis=-1, keepdims=True)
    i2 = jnp.min(jnp.where(rest == m2, lane, LANES), axis=-1, keepdims=True)
    e2 = jnp.exp(m2 - m1)
    w1 = 1.0 / (1.0 + e2)
    w2 = e2 / (1.0 + e2)
    mask = jnp.where(lane == i1, 1.0, jnp.where(lane == i2, 1.0, 0.0))
    comb = jnp.where(lane == i1, w1, jnp.where(lane == i2, w2, 0.0))
    return mask, comb


def _moe_kernel(x_ref, wr_ref, br_ref, tri_ref, wg_ref, wu_ref, wd_ref, g_ref, b_ref, o_ref,
                comb_ref, rank_ref, rankt_ref, cnt_ref, offs_ref, xy_ref):
    tile = x_ref.shape[0]
    n_sub = tile // MOE_SUB
    e = pl.program_id(1)
    lane = lax.broadcasted_iota(jnp.int32, (MOE_SUB, LANES), 1)
    slot_rows = lax.broadcasted_iota(jnp.int32, (MOE_SLOTS, MOE_SUB), 0).astype(F32)
    slot_cols = lax.broadcasted_iota(jnp.int32, (MOE_SUB, MOE_SUB), 1).astype(F32)

    @pl.when(jnp.logical_and(pl.program_id(0) == 0, e == 0))
    def _():
        xy_ref[...] = jnp.zeros_like(xy_ref)

    @pl.when(e == 0)
    def _():
        for sb in range(n_sub):
            rows = slice(sb * MOE_SUB, (sb + 1) * MOE_SUB)
            xb = x_ref[rows, :].astype(BF16)
            logits = jnp.dot(xb, wr_ref[...], preferred_element_type=F32) + br_ref[...]
            mask, comb = _route(logits, lane)
            comb_ref[rows, :] = comb
            rank = jnp.dot(tri_ref[...], mask.astype(BF16), preferred_element_type=F32)
            rank = jnp.where(mask > 0.0, rank, -1.0)
            rank_ref[rows, :] = rank
            rankt_ref[:, rows] = rank.T
            counts = jnp.sum(mask, axis=0, keepdims=True).astype(jnp.int32)
            for ee in range(N_EXPERTS):
                cnt_ref[sb * N_EXPERTS + ee] = counts[0, ee]
        o_ref[...] = jnp.zeros_like(o_ref)

    def gather(sb, first_slot):
        rows = slice(sb * MOE_SUB, (sb + 1) * MOE_SUB)
        slot_of_token = rankt_ref[pl.ds(e, 1), rows] - first_slot
        take = jnp.where(slot_rows == slot_of_token, 1.0, 0.0).astype(BF16)
        return jnp.dot(take, x_ref[rows, :].astype(BF16), preferred_element_type=F32)

    off = 0
    for sb in range(n_sub):
        xy_ref[pl.ds(pl.multiple_of(off, MOE_ALIGN), MOE_SLOTS), :] = gather(sb, 0.0).astype(BF16)
        offs_ref[sb] = off
        count = cnt_ref[sb * N_EXPERTS + e]
        off = off + ((count + (MOE_ALIGN - 1)) // MOE_ALIGN) * MOE_ALIGN
    n_rows = off

    for sb in range(n_sub):
        count = cnt_ref[sb * N_EXPERTS + e]

        @pl.when(count > MOE_SLOTS)
        def _():
            start = pl.multiple_of(offs_ref[sb] + MOE_SLOTS, MOE_ALIGN)
            row = lax.broadcasted_iota(jnp.int32, (MOE_SLOTS, D_MODEL), 0)
            rest = gather(sb, float(MOE_SLOTS))
            keep = xy_ref[pl.ds(start, MOE_SLOTS), :].astype(F32)
            merged = jnp.where(row < count - MOE_SLOTS, rest, keep)
            xy_ref[pl.ds(start, MOE_SLOTS), :] = merged.astype(BF16)

    def expert_rows(start, size):
        xc = xy_ref[pl.ds(start, size), :]
        y = _swiglu_partial(xc, wg_ref[0], wu_ref[0], wd_ref[0])
        xy_ref[pl.ds(start, size), :] = y.astype(BF16)

    n_full = n_rows // MOE_CHUNK

    def full_body(i, carry):
        expert_rows(pl.multiple_of(i * MOE_CHUNK, MOE_CHUNK), MOE_CHUNK)
        return carry

    lax.fori_loop(0, n_full, full_body, 0)
    n_tail = (n_rows - n_full * MOE_CHUNK + (MOE_TAIL - 1)) // MOE_TAIL

    def tail_body(j, carry):
        expert_rows(pl.multiple_of(n_full * MOE_CHUNK + j * MOE_TAIL, MOE_TAIL), MOE_TAIL)
        return carry

    lax.fori_loop(0, n_tail, tail_body, 0)

    for sb in range(n_sub):
        rows = slice(sb * MOE_SUB, (sb + 1) * MOE_SUB)
        here = lane == e
        slot = jnp.sum(jnp.where(here, rank_ref[rows, :], 0.0), axis=-1, keepdims=True)
        weight = jnp.sum(jnp.where(here, comb_ref[rows, :], 0.0), axis=-1, keepdims=True)
        put = jnp.where(slot_cols == slot, 1.0, 0.0).astype(BF16)
        yb = xy_ref[pl.ds(pl.multiple_of(offs_ref[sb], MOE_ALIGN), MOE_SUB), :]
        o_ref[rows, :] += weight * jnp.dot(put, yb, preferred_element_type=F32)

    @pl.when(e == N_EXPERTS - 1)
    def _():
        o_ref[...] = _layer_norm(ALPHA * x_ref[...] + o_ref[...], g_ref[...], b_ref[...])


def _moe_call(x, ew):
    n = x.shape[0]
    tile = min(MOE_TILE, n)
    n_sub = tile // MOE_SUB
    row_spec = lambda **kw: pl.BlockSpec((tile, D_MODEL), lambda i, e: (i, 0), **kw)
    up_spec = pl.BlockSpec((1, D_MODEL, D_FF_EXPERT), lambda i, e: (e, 0, 0))
    down_spec = pl.BlockSpec((1, D_FF_EXPERT, D_MODEL), lambda i, e: (e, 0, 0))
    consts = (ew["w_router"], ew["b_router"], ew["tri"])
    small = (ew["ln2_g"], ew["ln2_b"])
    return pl.pallas_call(
        _moe_kernel,
        grid=(n // tile, N_EXPERTS),
        in_specs=[row_spec(pipeline_mode=pl.Buffered(1))] + [_const_spec(c.shape) for c in consts]
        + [up_spec, up_spec, down_spec] + [_const_spec(c.shape) for c in small],
        out_specs=row_spec(),
        out_shape=jax.ShapeDtypeStruct((n, D_MODEL), F32),
        scratch_shapes=[pltpu.VMEM((tile, LANES), F32), pltpu.VMEM((tile, LANES), F32),
                        pltpu.VMEM((LANES, tile), F32),
                        pltpu.SMEM((n_sub * N_EXPERTS,), jnp.int32),
                        pltpu.SMEM((n_sub,), jnp.int32),
                        pltpu.VMEM((tile + 2 * MOE_SUB, D_MODEL), BF16)],
        compiler_params=pltpu.CompilerParams(dimension_semantics=("arbitrary", "arbitrary"),
                                             vmem_limit_bytes=VMEM_LIMIT_HIGH),
        name="experts",
    )(x, *consts, ew["w_gate"], ew["w_up"], ew["w_down"], *small)


def _rope_tables(seq):
    rows = seq // GRID_W
    row = jnp.repeat(jnp.arange(rows), GRID_W).astype(F32)
    col = jnp.tile(jnp.arange(GRID_W), rows).astype(F32)
    n_freq = HEAD_DIM // 4
    inv = ROPE_THETA ** (-jnp.arange(n_freq, dtype=F32) / n_freq)
    ang = jnp.concatenate([row[:, None] * inv] * 2 + [col[:, None] * inv] * 2, axis=1)
    ang = jnp.concatenate([ang] * (LANES // HEAD_DIM), axis=1)
    first = (jnp.arange(LANES) % (2 * n_freq)) < n_freq
    cos, sin = jnp.cos(ang), jnp.sin(ang)
    return cos, jnp.where(first, -sin, 0.0), jnp.where(first, 0.0, sin)


def _row(v):
    return v.reshape(1, -1).astype(F32)


def _prepare(p, seq):
    cos, sin_a, sin_b = _rope_tables(seq)
    head = jnp.arange(ATTN_W) // HEAD_DIM
    seg = (head[:, None] == head[None, :]).astype(BF16)
    layers = []
    for l in range(DEPTH):
        w_in = p["w_in"][l]
        front = dict(
            w_front=w_in[:, :FRONT_W].astype(BF16), cos=cos, sin_a=sin_a, sin_b=sin_b,
            q_gain=_row(jnp.tile(p["q_norm_g"][l], N_HEADS)),
            k_gain=_row(jnp.tile(p["k_norm_g"][l], N_KV_HEADS)),
            seg=seg,
            w_mix=jax.scipy.linalg.block_diag(*p["w_pool_mix"][l]).astype(BF16),
            pool_scale=_row(p["pool_scale"][l]), w_dw=p["w_dw"][l].astype(F32),
            b_dw=_row(p["b_dw"][l]), conv_ln_g=_row(p["conv_ln_g"][l]),
            conv_ln_b=_row(p["conv_ln_b"][l]))
        merge = dict(
            w_gate=w_in[:, FRONT_W:].astype(BF16), b_gate=_row(p["b_gate"][l]),
            w_attn_out=p["w_attn_out"][l].astype(BF16), w_pool_out=p["w_pool_out"][l].astype(BF16),
            w_conv_out=p["w_conv_out"][l].astype(BF16), w_out=p["w_out"][l].astype(BF16),
            ln1_g=_row(p["ln1_g"][l]), ln1_b=_row(p["ln1_b"][l]))
        j = l // 2
        if l % 2 == 0:
            mixer = dict(w_gate=p["w_ff_gate"][j].astype(BF16), w_up=p["w_ff_up"][j].astype(BF16),
                         w_down=p["w_ff_down"][j].astype(BF16))
        else:
            pad = LANES - N_EXPERTS
            sub = jnp.arange(MOE_SUB)
            mixer = dict(
                tri=(sub[None, :] < sub[:, None]).astype(BF16),
                w_router=jnp.pad(p["w_router"][j], ((0, 0), (0, pad))).astype(BF16),
                b_router=jnp.pad(_row(p["b_router"][j]), ((0, 0), (0, pad)),
                                 constant_values=NEG_BIG),
                w_gate=p["w_e_gate"][j].astype(BF16), w_up=p["w_e_up"][j].astype(BF16),
                w_down=p["w_e_down"][j].astype(BF16))
        mixer["ln2_g"] = _row(p["ln2_g"][l])
        mixer["ln2_b"] = _row(p["ln2_b"][l])
        layers.append((front, merge, mixer))
    return layers


def _trunk(x, layers):
    batch, seq, _ = x.shape
    for l, (front, merge, mixer) in enumerate(layers):
        q, k, v, pool, conv = _front_call(x, front)
        attn = _attn_call(q, k, v)
        flat = lambda a: a.reshape(batch * seq, a.shape[-1])
        x1 = _merge_call(flat(x), flat(attn), flat(pool), flat(conv), merge)
        x2 = _ffn_call(x1, mixer) if l % 2 == 0 else _moe_call(x1, mixer)
        x = x2.reshape(batch, seq, D_MODEL)
    return x


def kernel(x_prompt, x_sample, w_in, b_gate, q_norm_g, k_norm_g, w_attn_out, w_pool_mix, pool_scale,
           w_pool_out, w_dw, b_dw, conv_ln_g, conv_ln_b, w_conv_out, w_out, ln1_g, ln1_b,
           w_ff_gate, w_ff_up, w_ff_down, w_router, b_router, w_e_gate, w_e_up, w_e_down,
           ln2_g, ln2_b):
    params = dict(w_in=w_in, b_gate=b_gate, q_norm_g=q_norm_g, k_norm_g=k_norm_g,
                  w_attn_out=w_attn_out, w_pool_mix=w_pool_mix, pool_scale=pool_scale,
                  w_pool_out=w_pool_out, w_dw=w_dw, b_dw=b_dw, conv_ln_g=conv_ln_g,
                  conv_ln_b=conv_ln_b, w_conv_out=w_conv_out, w_out=w_out, ln1_g=ln1_g, ln1_b=ln1_b,
                  w_ff_gate=w_ff_gate, w_ff_up=w_ff_up, w_ff_down=w_ff_down, w_router=w_router,
                  b_router=b_router, w_e_gate=w_e_gate, w_e_up=w_e_up, w_e_down=w_e_down,
                  ln2_g=ln2_g, ln2_b=ln2_b)
    assert x_prompt.shape[1] == x_sample.shape[1]
    layers = _prepare(params, x_prompt.shape[1])
    return _trunk(x_prompt, layers), _trunk(x_sample, layers)
```

```python
import functools

import jax
import jax.numpy as jnp
from jax import lax
from jax.experimental import pallas as pl
from jax.experimental.pallas import tpu as pltpu

D_MODEL = 1024
DEPTH = 2
GRID_W = 64
N_HEADS = 8
N_KV_HEADS = 2
HEAD_DIM = 64
ATTN_W = N_HEADS * HEAD_DIM
KV_W = N_KV_HEADS * HEAD_DIM
ROPE_THETA = 10000.0
POOL_WINDOWS = (2, 4, 8, 16)
POOL_GROUP = 64
POOL_W = POOL_GROUP * len(POOL_WINDOWS)
CONV_W = 256
CONV_K = 31
N_BRANCH = 3
FRONT_W = ATTN_W + 2 * KV_W + POOL_W + 2 * CONV_W
D_FF = 2816
N_EXPERTS = 8
D_FF_EXPERT = 1408
ALPHA = (2 * DEPTH) ** 0.25
LN_EPS = 1e-5
RMS_EPS = 1e-6

LANES = 128
SUBLANES = 8
POOL_PAD = 8
CONV_PAD = 16
FRONT_ROWS = 256
CONV_ROWS = 128
ATTN_TQ = 512
LOG2_E = 1.4426950408889634
TOKEN_TILE = 512
MOE_TILE = 2048
MOE_SUB = 256
MOE_ROUTE_ROWS = 1024
MOE_ALIGN = 16
MOE_SLOTS = MOE_SUB // 2
MOE_CHUNK = 256
MOE_TAIL = 128
NEG_BIG = -1e30
VMEM_LIMIT = 56 * 1024 * 1024
VMEM_LIMIT_HIGH = 60 * 1024 * 1024

F32 = jnp.float32
BF16 = jnp.bfloat16


def _sigmoid(x):
    return 1.0 / (1.0 + jnp.exp(-x))


def _layer_norm(z, g, b):
    mu = jnp.mean(z, axis=-1, keepdims=True)
    d = z - mu
    var = jnp.mean(d * d, axis=-1, keepdims=True)
    return d * lax.rsqrt(var + LN_EPS) * g + b


def _const_spec(shape):
    zeros = (0,) * len(shape)
    return pl.BlockSpec(shape, lambda *_: zeros, pipeline_mode=pl.Buffered(1))


def _segment_sumsq(x, seg):
    sq = x * x
    hi = sq.astype(BF16)
    lo = (sq - hi.astype(F32)).astype(BF16)
    return (jnp.dot(hi, seg, preferred_element_type=F32)
            + jnp.dot(lo, seg, preferred_element_type=F32))


def _norm_rope(x, seg, gain, cos, sin_a, sin_b):
    width = x.shape[-1]
    ms = _segment_sumsq(x, seg) * (1.0 / HEAD_DIM)
    xn = x * lax.rsqrt(ms + RMS_EPS) * gain
    quarter = HEAD_DIM // 4
    up = pltpu.roll(xn, width - quarter, 1)
    down = pltpu.roll(xn, quarter, 1)
    return xn * cos + up * sin_a + down * sin_b


def _front_kernel(x_ref, w_ref, cos_ref, sina_ref, sinb_ref, qg_ref, kg_ref, seg_ref, wmix_ref,
                  pscale_ref, wdw_ref, bdw_ref, lng_ref, lnb_ref,
                  q_ref, k_ref, v_ref, pool_ref, conv_ref, ubuf, hbuf, hs_ref, cacc_ref):
    seq = x_ref.shape[1]
    rows = min(FRONT_ROWS, seq)
    o1, o2, o3 = ATTN_W, ATTN_W + KV_W, ATTN_W + 2 * KV_W
    o4, o5 = o3 + POOL_W, o3 + POOL_W + CONV_W

    ubuf[0:POOL_PAD, :] = jnp.zeros((POOL_PAD, POOL_W), F32)
    ubuf[POOL_PAD + seq:, :] = jnp.zeros((POOL_PAD, POOL_W), F32)
    hbuf[0:CONV_PAD, :] = jnp.zeros((CONV_PAD, CONV_W), F32)
    hbuf[CONV_PAD + seq:, :] = jnp.zeros((CONV_PAD, CONV_W), F32)

    def proj_chunk(c, carry):
        r0 = pl.multiple_of(c * rows, rows)
        xb = x_ref[0, pl.ds(r0, rows), :].astype(BF16)
        proj = jnp.dot(xb, w_ref[...], preferred_element_type=F32)
        cos = cos_ref[pl.ds(r0, rows), :]
        sin_a = sina_ref[pl.ds(r0, rows), :]
        sin_b = sinb_ref[pl.ds(r0, rows), :]
        rep = ATTN_W // LANES
        q = _norm_rope(proj[:, :o1], seg_ref[...], qg_ref[...],
                       jnp.concatenate([cos] * rep, axis=1),
                       jnp.concatenate([sin_a] * rep, axis=1),
                       jnp.concatenate([sin_b] * rep, axis=1))
        q_ref[0, pl.ds(r0, rows), :] = (q * (HEAD_DIM ** -0.5 * LOG2_E)).astype(BF16)
        k = _norm_rope(proj[:, o1:o2], seg_ref[0:KV_W, 0:KV_W], kg_ref[...], cos, sin_a, sin_b)
        k_ref[0, pl.ds(r0, rows), :] = k.astype(BF16)
        v = proj[:, o2:o3]
        low = lax.broadcasted_iota(jnp.int32, v.shape, 1) < HEAD_DIM
        v_ones = [jnp.where(low, v, 1.0), jnp.where(low, pltpu.roll(v, HEAD_DIM, 1), 1.0)]
        v_ref[0, pl.ds(r0, rows), :] = jnp.concatenate(v_ones, axis=1).astype(BF16)
        ubuf[pl.ds(POOL_PAD + r0, rows), :] = proj[:, o3:o4]
        hbuf[pl.ds(CONV_PAD + r0, rows), :] = proj[:, o4:o5] * _sigmoid(proj[:, o5:])
        return carry

    lax.fori_loop(0, seq // rows, proj_chunk, 0)

    lane = lax.broadcasted_iota(jnp.int32, (rows, LANES), 1)
    first = lane < POOL_GROUP
    for c in range(seq // rows):
        r0 = c * rows
        t = r0 + lax.broadcasted_iota(jnp.int32, (rows, LANES), 0)
        halves = []
        for half in range(POOL_W // LANES):
            w_a, w_b = POOL_WINDOWS[2 * half], POOL_WINDOWS[2 * half + 1]
            lo_a, hi_a = w_a // 2, w_a - w_a // 2 - 1
            lo_b, hi_b = w_b // 2, w_b - w_b // 2 - 1
            cols = slice(half * LANES, (half + 1) * LANES)

            def shifted(off):
                return ubuf[POOL_PAD + r0 + off:POOL_PAD + r0 + off + rows, cols]

            centre = shifted(0)
            tot_a = centre
            for off in range(-lo_a, hi_a + 1):
                if off != 0:
                    tot_a = tot_a + shifted(off)
            tot_b = tot_a
            for off in range(-lo_b, hi_b + 1):
                if off < -lo_a or off > hi_a:
                    tot_b = tot_b + shifted(off)
            lo = jnp.where(first, lo_a, lo_b)
            hi = jnp.where(first, hi_a, hi_b)
            cnt = (jnp.minimum(t + hi + 1, seq) - jnp.maximum(t - lo, 0)).astype(F32)
            tot = jnp.where(first, tot_a, tot_b)
            halves.append(tot / cnt - centre)
        pooled = jnp.concatenate(halves, axis=1).astype(BF16)
        mixed = jnp.dot(pooled, wmix_ref[...], preferred_element_type=F32) * pscale_ref[...]
        pool_ref[0, r0:r0 + rows, :] = mixed.astype(BF16)

    crow = min(CONV_ROWS, seq)
    shift0 = CONV_PAD - CONV_K // 2
    tail_rows = 2 * CONV_PAD - SUBLANES
    n_half = CONV_W // LANES
    for half in range(n_half):
        cols = slice(half * LANES, (half + 1) * LANES)

        def shift_chunk(c, carry):
            j0 = pl.multiple_of(c * crow, crow)
            win = hbuf[pl.ds(j0, crow + SUBLANES), cols]
            for s in range(1, SUBLANES):
                hs_ref[s - 1, pl.ds(j0, crow), :] = win[s:s + crow, :]
            return carry

        lax.fori_loop(0, seq // crow, shift_chunk, 0)
        tail = hbuf[seq:seq + 2 * CONV_PAD, cols]
        for s in range(1, SUBLANES):
            hs_ref[s - 1, seq:seq + tail_rows, :] = tail[s:s + tail_rows, :]

        def tap_chunk(c, carry):
            r0 = pl.multiple_of(c * crow, crow)
            acc = jnp.broadcast_to(bdw_ref[:, cols], (crow, LANES))
            for kk in range(CONV_K):
                a, s = divmod(kk + shift0, SUBLANES)
                start = r0 + SUBLANES * a
                src = hbuf[pl.ds(start, crow), cols] if s == 0 else hs_ref[s - 1, pl.ds(start, crow), :]
                acc = acc + src * wdw_ref[kk:kk + 1, cols]
            cacc_ref[pl.ds(r0, crow), cols] = acc
            return carry

        lax.fori_loop(0, seq // crow, tap_chunk, 0)

    def norm_chunk(c, carry):
        r0 = pl.multiple_of(c * rows, rows)
        hn = _layer_norm(cacc_ref[pl.ds(r0, rows), :], lng_ref[...], lnb_ref[...])
        conv_ref[0, pl.ds(r0, rows), :] = (hn * _sigmoid(hn)).astype(BF16)
        return carry

    lax.fori_loop(0, seq // rows, norm_chunk, 0)


def _front_call(x, fw):
    batch, seq, _ = x.shape
    seq_spec = lambda w: pl.BlockSpec((1, seq, w), lambda b: (b, 0, 0))
    out_shape = (
        jax.ShapeDtypeStruct((batch, seq, ATTN_W), BF16),
        jax.ShapeDtypeStruct((batch, seq, KV_W), BF16),
        jax.ShapeDtypeStruct((batch, seq, 2 * KV_W), BF16),
        jax.ShapeDtypeStruct((batch, seq, POOL_W), BF16),
        jax.ShapeDtypeStruct((batch, seq, CONV_W), BF16),
    )
    consts = (fw["w_front"], fw["cos"], fw["sin_a"], fw["sin_b"], fw["q_gain"], fw["k_gain"],
              fw["seg"], fw["w_mix"], fw["pool_scale"], fw["w_dw"], fw["b_dw"], fw["conv_ln_g"],
              fw["conv_ln_b"])
    return pl.pallas_call(
        _front_kernel,
        grid=(batch,),
        in_specs=[seq_spec(D_MODEL)] + [_const_spec(c.shape) for c in consts],
        out_specs=(seq_spec(ATTN_W), seq_spec(KV_W), seq_spec(2 * KV_W), seq_spec(POOL_W),
                   seq_spec(CONV_W)),
        out_shape=out_shape,
        scratch_shapes=[pltpu.VMEM((seq + 2 * POOL_PAD, POOL_W), F32),
                        pltpu.VMEM((seq + 2 * CONV_PAD, CONV_W), F32),
                        pltpu.VMEM((SUBLANES - 1, seq + 2 * CONV_PAD, LANES), F32),
                        pltpu.VMEM((seq, CONV_W), F32)],
        compiler_params=pltpu.CompilerParams(dimension_semantics=("arbitrary",),
                                             vmem_limit_bytes=VMEM_LIMIT),
        name="front",
    )(x, *consts)


def _attn_kernel(q_ref, k_ref, v_ref, o_ref):
    group = N_HEADS // N_KV_HEADS
    for kv in range(N_KV_HEADS):
        k = k_ref[0, :, kv * HEAD_DIM:(kv + 1) * HEAD_DIM]
        v_ones = v_ref[0, :, 2 * kv * HEAD_DIM:2 * (kv + 1) * HEAD_DIM]
        for g in range(group):
            hcols = slice((kv * group + g) * HEAD_DIM, (kv * group + g + 1) * HEAD_DIM)
            q = q_ref[0, :, hcols]
            s = lax.dot_general(q, k, (((1,), (1,)), ((), ())), preferred_element_type=F32)
            m = jnp.max(s, axis=-1, keepdims=True)
            p = jnp.exp2(s - m).astype(BF16)
            o = jnp.dot(p, v_ones, preferred_element_type=F32)
            o = o / pltpu.roll(o, HEAD_DIM, 1)
            o_ref[0, :, hcols] = o[:, :HEAD_DIM].astype(BF16)


def _attn_call(q, k, v_ones):
    batch, seq, _ = q.shape
    tq = min(ATTN_TQ, seq)
    return pl.pallas_call(
        _attn_kernel,
        grid=(batch, seq // tq),
        in_specs=[pl.BlockSpec((1, tq, ATTN_W), lambda b, i: (b, i, 0)),
                  pl.BlockSpec((1, seq, KV_W), lambda b, i: (b, 0, 0)),
                  pl.BlockSpec((1, seq, 2 * KV_W), lambda b, i: (b, 0, 0))],
        out_specs=pl.BlockSpec((1, tq, ATTN_W), lambda b, i: (b, i, 0)),
        out_shape=jax.ShapeDtypeStruct((batch, seq, ATTN_W), BF16),
        compiler_params=pltpu.CompilerParams(dimension_semantics=("arbitrary", "arbitrary"),
                                             vmem_limit_bytes=VMEM_LIMIT),
        name="attention",
    )(q, k, v_ones)


def _merge_kernel(x_ref, a_ref, p_ref, c_ref, wg_ref, bg_ref, wa_ref, wp_ref, wc_ref, wo_ref,
                  g_ref, b_ref, o_ref):
    x = x_ref[...]
    xb = x.astype(BF16)
    merged = None
    for br, (in_ref, w_ref) in enumerate(((a_ref, wa_ref), (p_ref, wp_ref), (c_ref, wc_ref))):
        cols = slice(br * D_MODEL, (br + 1) * D_MODEL)
        gate = _sigmoid(jnp.dot(xb, wg_ref[:, cols], preferred_element_type=F32) + bg_ref[:, cols])
        term = gate * jnp.dot(in_ref[...], w_ref[...], preferred_element_type=F32)
        merged = term if merged is None else merged + term
    mix = jnp.dot(merged.astype(BF16), wo_ref[...], preferred_element_type=F32)
    o_ref[...] = _layer_norm(ALPHA * x + mix, g_ref[...], b_ref[...])


def _merge_call(x, attn, pool, conv, mw):
    n = x.shape[0]
    tm = min(TOKEN_TILE, n)
    row_spec = lambda w: pl.BlockSpec((tm, w), lambda i: (i, 0))
    consts = (mw["w_gate"], mw["b_gate"], mw["w_attn_out"], mw["w_pool_out"], mw["w_conv_out"],
              mw["w_out"], mw["ln1_g"], mw["ln1_b"])
    return pl.pallas_call(
        _merge_kernel,
        grid=(n // tm,),
        in_specs=[row_spec(D_MODEL), row_spec(ATTN_W), row_spec(POOL_W), row_spec(CONV_W)]
        + [_const_spec(c.shape) for c in consts],
        out_specs=row_spec(D_MODEL),
        out_shape=jax.ShapeDtypeStruct((n, D_MODEL), F32),
        compiler_params=pltpu.CompilerParams(dimension_semantics=("arbitrary",),
                                             vmem_limit_bytes=VMEM_LIMIT),
        name="merge",
    )(x, attn, pool, conv, *consts)


def _swiglu_partial(xb, wg, wu, wd):
    hg = jnp.dot(xb, wg, preferred_element_type=F32)
    hu = jnp.dot(xb, wu, preferred_element_type=F32)
    h = (hg * _sigmoid(hg) * hu).astype(BF16)
    return jnp.dot(h, wd, preferred_element_type=F32)


def _ffn_kernel(x_ref, wg_ref, wu_ref, wd_ref, g_ref, b_ref, o_ref):
    x = x_ref[...]
    xb = x.astype(BF16)
    half = D_FF // 2
    f = None
    for c in range(2):
        cols = slice(c * half, (c + 1) * half)
        part = _swiglu_partial(xb, wg_ref[:, cols], wu_ref[:, cols], wd_ref[cols, :])
        f = part if f is None else f + part
    o_ref[...] = _layer_norm(ALPHA * x + f, g_ref[...], b_ref[...])


def _ffn_call(x, dw):
    n = x.shape[0]
    tm = min(TOKEN_TILE, n)
    row_spec = pl.BlockSpec((tm, D_MODEL), lambda i: (i, 0))
    consts = (dw["w_gate"], dw["w_up"], dw["w_down"], dw["ln2_g"], dw["ln2_b"])
    return pl.pallas_call(
        _ffn_kernel,
        grid=(n // tm,),
        in_specs=[row_spec] + [_const_spec(c.shape) for c in consts],
        out_specs=row_spec,
        out_shape=jax.ShapeDtypeStruct((n, D_MODEL), F32),
        compiler_params=pltpu.CompilerParams(dimension_semantics=("arbitrary",),
                                             vmem_limit_bytes=VMEM_LIMIT),
        name="ffn",
    )(x, *consts)


def _route(logits, lane):
    m1 = jnp.max(logits, axis=-1, keepdims=True)
    i1 = jnp.min(jnp.where(logits == m1, lane, LANES), axis=-1, keepdims=True)
    rest = jnp.where(lane == i1, NEG_BIG, logits)
    m2 = jnp.max(rest, axis=-1, keepdims=True)
    i2 = jnp.min(jnp.where(rest == m2, lane, LANES), axis=-1, keepdims=True)
    e2 = jnp.exp(m2 - m1)
    w1 = 1.0 / (1.0 + e2)
    w2 = e2 / (1.0 + e2)
    mask = jnp.where(lane == i1, 1.0, jnp.where(lane == i2, 1.0, 0.0))
    comb = jnp.where(lane == i1, w1, jnp.where(lane == i2, w2, 0.0))
    return mask, comb


def _moe_kernel(x_ref, wr_ref, br_ref, tri_ref, wg_ref, wu_ref, wd_ref, g_ref, b_ref, o_ref,
                comb_ref, rank_ref, rankt_ref, cnt_ref, offs_ref, xy_ref):
    tile = x_ref.shape[0]
    n_sub = tile // MOE_SUB
    e = pl.program_id(1)
    lane = lax.broadcasted_iota(jnp.int32, (MOE_SUB, LANES), 1)
    slot_rows = lax.broadcasted_iota(jnp.int32, (MOE_SLOTS, MOE_SUB), 0).astype(F32)
    slot_cols = lax.broadcasted_iota(jnp.int32, (MOE_SUB, MOE_SUB), 1).astype(F32)

    @pl.when(jnp.logical_and(pl.program_id(0) == 0, e == 0))
    def _():
        xy_ref[...] = jnp.zeros_like(xy_ref)

    @pl.when(e == 0)
    def _():
        for sb in range(n_sub):
            rows = slice(sb * MOE_SUB, (sb + 1) * MOE_SUB)
            xb = x_ref[rows, :].astype(BF16)
            comb_ref[rows, :] = jnp.dot(xb, wr_ref[...], preferred_element_type=F32) + br_ref[...]
        part = min(MOE_ROUTE_ROWS, tile)
        part_lane = lax.broadcasted_iota(jnp.int32, (part, LANES), 1)
        for c in range(tile // part):
            rows = slice(c * part, (c + 1) * part)
            mask, comb = _route(comb_ref[rows, :], part_lane)
            comb_ref[rows, :] = comb
            rank_ref[rows, :] = mask
        for sb in range(n_sub):
            rows = slice(sb * MOE_SUB, (sb + 1) * MOE_SUB)
            mask = rank_ref[rows, :]
            rank = jnp.dot(tri_ref[...], mask.astype(BF16), preferred_element_type=F32)
            rank = jnp.where(mask > 0.0, rank, -1.0)
            rank_ref[rows, :] = rank
            rankt_ref[:, rows] = rank.T
            counts = jnp.sum(mask, axis=0, keepdims=True).astype(jnp.int32)
            for ee in range(N_EXPERTS):
                cnt_ref[sb * N_EXPERTS + ee] = counts[0, ee]
        o_ref[...] = jnp.zeros_like(o_ref)

    def gather(sb, first_slot):
        rows = slice(sb * MOE_SUB, (sb + 1) * MOE_SUB)
        slot_of_token = rankt_ref[pl.ds(e, 1), rows] - first_slot
        take = jnp.where(slot_rows == slot_of_token, 1.0, 0.0).astype(BF16)
        return jnp.dot(take, x_ref[rows, :].astype(BF16), preferred_element_type=F32)

    off = 0
    for sb in range(n_sub):
        xy_ref[pl.ds(pl.multiple_of(off, MOE_ALIGN), MOE_SLOTS), :] = gather(sb, 0.0).astype(BF16)
        offs_ref[sb] = off
        count = cnt_ref[sb * N_EXPERTS + e]
        off = off + ((count + (MOE_ALIGN - 1)) // MOE_ALIGN) * MOE_ALIGN
    n_rows = off

    for sb in range(n_sub):
        count = cnt_ref[sb * N_EXPERTS + e]

        @pl.when(count > MOE_SLOTS)
        def _():
            start = pl.multiple_of(offs_ref[sb] + MOE_SLOTS, MOE_ALIGN)
            row = lax.broadcasted_iota(jnp.int32, (MOE_SLOTS, D_MODEL), 0)
            rest = gather(sb, float(MOE_SLOTS))
            keep = xy_ref[pl.ds(start, MOE_SLOTS), :].astype(F32)
            merged = jnp.where(row < count - MOE_SLOTS, rest, keep)
            xy_ref[pl.ds(start, MOE_SLOTS), :] = merged.astype(BF16)

    def expert_rows(start, size):
        xc = xy_ref[pl.ds(start, size), :]
        y = _swiglu_partial(xc, wg_ref[0], wu_ref[0], wd_ref[0])
        xy_ref[pl.ds(start, size), :] = y.astype(BF16)

    n_full = n_rows // MOE_CHUNK

    def full_body(i, carry):
        expert_rows(pl.multiple_of(i * MOE_CHUNK, MOE_CHUNK), MOE_CHUNK)
        return carry

    lax.fori_loop(0, n_full, full_body, 0)
    n_tail = (n_rows - n_full * MOE_CHUNK + (MOE_TAIL - 1)) // MOE_TAIL

    def tail_body(j, carry):
        expert_rows(pl.multiple_of(n_full * MOE_CHUNK + j * MOE_TAIL, MOE_TAIL), MOE_TAIL)
        return carry

    lax.fori_loop(0, n_tail, tail_body, 0)

    for sb in range(n_sub):
        rows = slice(sb * MOE_SUB, (sb + 1) * MOE_SUB)
        here = lane == e
        slot = jnp.sum(jnp.where(here, rank_ref[rows, :], 0.0), axis=-1, keepdims=True)
        weight = jnp.sum(jnp.where(here, comb_ref[rows, :], 0.0), axis=-1, keepdims=True)
        put = jnp.where(slot_cols == slot, 1.0, 0.0).astype(BF16)
        yb = xy_ref[pl.ds(pl.multiple_of(offs_ref[sb], MOE_ALIGN), MOE_SUB), :]
        o_ref[rows, :] += weight * jnp.dot(put, yb, preferred_element_type=F32)

    @pl.when(e == N_EXPERTS - 1)
    def _():
        o_ref[...] = _layer_norm(ALPHA * x_ref[...] + o_ref[...], g_ref[...], b_ref[...])


def _moe_call(x, ew):
    n = x.shape[0]
    tile = min(MOE_TILE, n)
    n_sub = tile // MOE_SUB
    row_spec = lambda **kw: pl.BlockSpec((tile, D_MODEL), lambda i, e: (i, 0), **kw)
    up_spec = pl.BlockSpec((1, D_MODEL, D_FF_EXPERT), lambda i, e: (e, 0, 0))
    down_spec = pl.BlockSpec((1, D_FF_EXPERT, D_MODEL), lambda i, e: (e, 0, 0))
    consts = (ew["w_router"], ew["b_router"], ew["tri"])
    small = (ew["ln2_g"], ew["ln2_b"])
    return pl.pallas_call(
        _moe_kernel,
        grid=(n // tile, N_EXPERTS),
        in_specs=[row_spec(pipeline_mode=pl.Buffered(1))] + [_const_spec(c.shape) for c in consts]
        + [up_spec, up_spec, down_spec] + [_const_spec(c.shape) for c in small],
        out_specs=row_spec(),
        out_shape=jax.ShapeDtypeStruct((n, D_MODEL), F32),
        scratch_shapes=[pltpu.VMEM((tile, LANES), F32), pltpu.VMEM((tile, LANES), F32),
                        pltpu.VMEM((LANES, tile), F32),
                        pltpu.SMEM((n_sub * N_EXPERTS,), jnp.int32),
                        pltpu.SMEM((n_sub,), jnp.int32),
                        pltpu.VMEM((tile + 2 * MOE_SUB, D_MODEL), BF16)],
        compiler_params=pltpu.CompilerParams(dimension_semantics=("arbitrary", "arbitrary"),
                                             vmem_limit_bytes=VMEM_LIMIT_HIGH),
        name="experts",
    )(x, *consts, ew["w_gate"], ew["w_up"], ew["w_down"], *small)


def _rope_tables(seq):
    rows = seq // GRID_W
    row = jnp.repeat(jnp.arange(rows), GRID_W).astype(F32)
    col = jnp.tile(jnp.arange(GRID_W), rows).astype(F32)
    n_freq = HEAD_DIM // 4
    inv = ROPE_THETA ** (-jnp.arange(n_freq, dtype=F32) / n_freq)
    ang = jnp.concatenate([row[:, None] * inv] * 2 + [col[:, None] * inv] * 2, axis=1)
    ang = jnp.concatenate([ang] * (LANES // HEAD_DIM), axis=1)
    first = (jnp.arange(LANES) % (2 * n_freq)) < n_freq
    cos, sin = jnp.cos(ang), jnp.sin(ang)
    return cos, jnp.where(first, -sin, 0.0), jnp.where(first, 0.0, sin)


def _row(v):
    return v.reshape(1, -1).astype(F32)


def _prepare(p, seq):
    cos, sin_a, sin_b = _rope_tables(seq)
    head = jnp.arange(ATTN_W) // HEAD_DIM
    seg = (head[:, None] == head[None, :]).astype(BF16)
    layers = []
    for l in range(DEPTH):
        w_in = p["w_in"][l]
        front = dict(
            w_front=w_in[:, :FRONT_W].astype(BF16), cos=cos, sin_a=sin_a, sin_b=sin_b,
            q_gain=_row(jnp.tile(p["q_norm_g"][l], N_HEADS)),
            k_gain=_row(jnp.tile(p["k_norm_g"][l], N_KV_HEADS)),
            seg=seg,
            w_mix=jax.scipy.linalg.block_diag(*p["w_pool_mix"][l]).astype(BF16),
            pool_scale=_row(p["pool_scale"][l]), w_dw=p["w_dw"][l].astype(F32),
            b_dw=_row(p["b_dw"][l]), conv_ln_g=_row(p["conv_ln_g"][l]),
            conv_ln_b=_row(p["conv_ln_b"][l]))
        merge = dict(
            w_gate=w_in[:, FRONT_W:].astype(BF16), b_gate=_row(p["b_gate"][l]),
            w_attn_out=p["w_attn_out"][l].astype(BF16), w_pool_out=p["w_pool_out"][l].astype(BF16),
            w_conv_out=p["w_conv_out"][l].astype(BF16), w_out=p["w_out"][l].astype(BF16),
            ln1_g=_row(p["ln1_g"][l]), ln1_b=_row(p["ln1_b"][l]))
        j = l // 2
        if l % 2 == 0:
            mixer = dict(w_gate=p["w_ff_gate"][j].astype(BF16), w_up=p["w_ff_up"][j].astype(BF16),
                         w_down=p["w_ff_down"][j].astype(BF16))
        else:
            pad = LANES - N_EXPERTS
            sub = jnp.arange(MOE_SUB)
            mixer = dict(
                tri=(sub[None, :] < sub[:, None]).astype(BF16),
                w_router=jnp.pad(p["w_router"][j], ((0, 0), (0, pad))).astype(BF16),
                b_router=jnp.pad(_row(p["b_router"][j]), ((0, 0), (0, pad)),
                                 constant_values=NEG_BIG),
                w_gate=p["w_e_gate"][j].astype(BF16), w_up=p["w_e_up"][j].astype(BF16),
                w_down=p["w_e_down"][j].astype(BF16))
        mixer["ln2_g"] = _row(p["ln2_g"][l])
        mixer["ln2_b"] = _row(p["ln2_b"][l])
        layers.append((front, merge, mixer))
    return layers


def _trunk(x, layers):
    batch, seq, _ = x.shape
    for l, (front, merge, mixer) in enumerate(layers):
        q, k, v, pool, conv = _front_call(x, front)
        attn = _attn_call(q, k, v)
        flat = lambda a: a.reshape(batch * seq, a.shape[-1])
        x1 = _merge_call(flat(x), flat(attn), flat(pool), flat(conv), merge)
        x2 = _ffn_call(x1, mixer) if l % 2 == 0 else _moe_call(x1, mixer)
        x = x2.reshape(batch, seq, D_MODEL)
    return x


def kernel(x_prompt, x_sample, w_in, b_gate, q_norm_g, k_norm_g, w_attn_out, w_pool_mix, pool_scale,
           w_pool_out, w_dw, b_dw, conv_ln_g, conv_ln_b, w_conv_out, w_out, ln1_g, ln1_b,
           w_ff_gate, w_ff_up, w_ff_down, w_router, b_router, w_e_gate, w_e_up, w_e_down,
           ln2_g, ln2_b):
    params = dict(w_in=w_in, b_gate=b_gate, q_norm_g=q_norm_g, k_norm_g=k_norm_g,
                  w_attn_out=w_attn_out, w_pool_mix=w_pool_mix, pool_scale=pool_scale,
                  w_pool_out=w_pool_out, w_dw=w_dw, b_dw=b_dw, conv_ln_g=conv_ln_g,
                  conv_ln_b=conv_ln_b, w_conv_out=w_conv_out, w_out=w_out, ln1_g=ln1_g, ln1_b=ln1_b,
                  w_ff_gate=w_ff_gate, w_ff_up=w_ff_up, w_ff_down=w_ff_down, w_router=w_router,
                  b_router=b_router, w_e_gate=w_e_gate, w_e_up=w_e_up, w_e_down=w_e_down,
                  ln2_g=ln2_g, ln2_b=ln2_b)
    assert x_prompt.shape[1] == x_sample.shape[1]
    layers = _prepare(params, x_prompt.shape[1])
    return _trunk(x_prompt, layers), _trunk(x_sample, layers)
```

```python
import math

import jax
import jax.numpy as jnp
from jax import lax
from jax.experimental import pallas as pl
from jax.experimental.pallas import tpu as pltpu

D_MODEL = 1024
DEPTH = 2
GRID_W = 64
N_HEADS = 8
N_KV_HEADS = 2
HEAD_DIM = 64
ATTN_W = N_HEADS * HEAD_DIM
KV_W = N_KV_HEADS * HEAD_DIM
ROPE_THETA = 10000.0
POOL_WINDOWS = (2, 4, 8, 16)
POOL_GROUP = 64
POOL_W = POOL_GROUP * len(POOL_WINDOWS)
CONV_W = 256
CONV_K = 31
N_BRANCH = 3
FRONT_W = ATTN_W + 2 * KV_W + POOL_W + 2 * CONV_W
D_FF = 2816
N_EXPERTS = 8
D_FF_EXPERT = 1408
ALPHA = (2 * DEPTH) ** 0.25
LN_EPS = 1e-5
RMS_EPS = 1e-6

LANES = 128
SUBLANES = 8
POOL_PAD = 8
CONV_PAD = 16
FRONT_ROWS = 256
FRONT_UNROLL = 4
CONV_ROWS = 128
ATTN_TQ = 512
LOG2_E = 1.4426950408889634
TOKEN_TILE = 512
ROW_GROUP = 256
MOE_TILE = 2048
MOE_SUB = 256
MOE_ROUTE_ROWS = 1024
MOE_ALIGN = 16
MOE_SLOTS = MOE_SUB // 2
MOE_CHUNK = 256
MOE_TAIL = 128
NEG_BIG = -1e30
VMEM_LIMIT = 56 * 1024 * 1024
VMEM_LIMIT_HIGH = 60 * 1024 * 1024

F32 = jnp.float32
BF16 = jnp.bfloat16


def _sigmoid(x):
    return 1.0 / (1.0 + jnp.exp(-x))


def _layer_norm(z, g, b):
    mu = jnp.mean(z, axis=-1, keepdims=True)
    d = z - mu
    var = jnp.mean(d * d, axis=-1, keepdims=True)
    return d * lax.rsqrt(var + LN_EPS) * g + b


def _const_spec(shape):
    zeros = (0,) * len(shape)
    return pl.BlockSpec(shape, lambda *_: zeros, pipeline_mode=pl.Buffered(1))


def _segment_sumsq(x, seg):
    sq = x * x
    hi = sq.astype(BF16)
    lo = (sq - hi.astype(F32)).astype(BF16)
    return (jnp.dot(hi, seg, preferred_element_type=F32)
            + jnp.dot(lo, seg, preferred_element_type=F32))


def _norm_rope(x, seg, gain, cos, sin_a, sin_b):
    width = x.shape[-1]
    ms = _segment_sumsq(x, seg) * (1.0 / HEAD_DIM)
    xn = x * lax.rsqrt(ms + RMS_EPS) * gain
    quarter = HEAD_DIM // 4
    up = pltpu.roll(xn, width - quarter, 1)
    down = pltpu.roll(xn, quarter, 1)
    return xn * cos + up * sin_a + down * sin_b


def _front_kernel(x_ref, w_ref, cos_ref, sina_ref, sinb_ref, qg_ref, kg_ref, seg_ref, wmix_ref,
                  pscale_ref, wdw_ref, bdw_ref, lng_ref, lnb_ref,
                  q_ref, k_ref, v_ref, pool_ref, conv_ref, ubuf, hbuf, hs_ref, cacc_ref):
    seq = x_ref.shape[1]
    rows = min(FRONT_ROWS, seq)
    o1, o2, o3 = ATTN_W, ATTN_W + KV_W, ATTN_W + 2 * KV_W
    o4, o5 = o3 + POOL_W, o3 + POOL_W + CONV_W

    ubuf[0:POOL_PAD, :] = jnp.zeros((POOL_PAD, POOL_W), F32)
    ubuf[POOL_PAD + seq:, :] = jnp.zeros((POOL_PAD, POOL_W), F32)
    hbuf[0:CONV_PAD, :] = jnp.zeros((CONV_PAD, CONV_W), F32)
    hbuf[CONV_PAD + seq:, :] = jnp.zeros((CONV_PAD, CONV_W), F32)

    def proj_chunk(c, carry):
        r0 = pl.multiple_of(c * rows, rows)
        xb = x_ref[0, pl.ds(r0, rows), :].astype(BF16)
        proj = jnp.dot(xb, w_ref[...], preferred_element_type=F32)
        cos = cos_ref[pl.ds(r0, rows), :]
        sin_a = sina_ref[pl.ds(r0, rows), :]
        sin_b = sinb_ref[pl.ds(r0, rows), :]
        rep = ATTN_W // LANES
        q = _norm_rope(proj[:, :o1], seg_ref[...], qg_ref[...],
                       jnp.concatenate([cos] * rep, axis=1),
                       jnp.concatenate([sin_a] * rep, axis=1),
                       jnp.concatenate([sin_b] * rep, axis=1))
        q_ref[0, pl.ds(r0, rows), :] = (q * (HEAD_DIM ** -0.5 * LOG2_E)).astype(BF16)
        k = _norm_rope(proj[:, o1:o2], seg_ref[0:KV_W, 0:KV_W], kg_ref[...], cos, sin_a, sin_b)
        k_ref[0, pl.ds(r0, rows), :] = k.astype(BF16)
        v = proj[:, o2:o3]
        low = lax.broadcasted_iota(jnp.int32, v.shape, 1) < HEAD_DIM
        v_ones = [jnp.where(low, v, 1.0), jnp.where(low, pltpu.roll(v, HEAD_DIM, 1), 1.0)]
        v_ref[0, pl.ds(r0, rows), :] = jnp.concatenate(v_ones, axis=1).astype(BF16)
        ubuf[pl.ds(POOL_PAD + r0, rows), :] = proj[:, o3:o4]
        hbuf[pl.ds(CONV_PAD + r0, rows), :] = proj[:, o4:o5] * _sigmoid(proj[:, o5:])
        return carry

    n_chunks = seq // rows
    lax.fori_loop(0, n_chunks, proj_chunk, 0, unroll=math.gcd(n_chunks, FRONT_UNROLL))

    lane = lax.broadcasted_iota(jnp.int32, (rows, LANES), 1)
    first = lane < POOL_GROUP
    for c in range(seq // rows):
        r0 = c * rows
        t = r0 + lax.broadcasted_iota(jnp.int32, (rows, LANES), 0)
        halves = []
        for half in range(POOL_W // LANES):
            w_a, w_b = POOL_WINDOWS[2 * half], POOL_WINDOWS[2 * half + 1]
            lo_a, hi_a = w_a // 2, w_a - w_a // 2 - 1
            lo_b, hi_b = w_b // 2, w_b - w_b // 2 - 1
            cols = slice(half * LANES, (half + 1) * LANES)

            def shifted(off):
                return ubuf[POOL_PAD + r0 + off:POOL_PAD + r0 + off + rows, cols]

            centre = shifted(0)
            tot_a = centre
            for off in range(-lo_a, hi_a + 1):
                if off != 0:
                    tot_a = tot_a + shifted(off)
            tot_b = tot_a
            for off in range(-lo_b, hi_b + 1):
                if off < -lo_a or off > hi_a:
                    tot_b = tot_b + shifted(off)
            lo = jnp.where(first, lo_a, lo_b)
            hi = jnp.where(first, hi_a, hi_b)
            cnt = (jnp.minimum(t + hi + 1, seq) - jnp.maximum(t - lo, 0)).astype(F32)
            tot = jnp.where(first, tot_a, tot_b)
            halves.append(tot / cnt - centre)
        pooled = jnp.concatenate(halves, axis=1).astype(BF16)
        mixed = jnp.dot(pooled, wmix_ref[...], preferred_element_type=F32) * pscale_ref[...]
        pool_ref[0, r0:r0 + rows, :] = mixed.astype(BF16)

    crow = min(CONV_ROWS, seq)
    shift0 = CONV_PAD - CONV_K // 2
    tail_rows = 2 * CONV_PAD - SUBLANES
    n_half = CONV_W // LANES
    for half in range(n_half):
        cols = slice(half * LANES, (half + 1) * LANES)

        def shift_chunk(c, carry):
            j0 = pl.multiple_of(c * crow, crow)
            win = hbuf[pl.ds(j0, crow + SUBLANES), cols]
            for s in range(1, SUBLANES):
                hs_ref[s - 1, pl.ds(j0, crow), :] = win[s:s + crow, :]
            return carry

        lax.fori_loop(0, seq // crow, shift_chunk, 0)
        tail = hbuf[seq:seq + 2 * CONV_PAD, cols]
        for s in range(1, SUBLANES):
            hs_ref[s - 1, seq:seq + tail_rows, :] = tail[s:s + tail_rows, :]

        def tap_chunk(c, carry):
            r0 = pl.multiple_of(c * crow, crow)
            acc = jnp.broadcast_to(bdw_ref[:, cols], (crow, LANES))
            for kk in range(CONV_K):
                a, s = divmod(kk + shift0, SUBLANES)
                start = r0 + SUBLANES * a
                src = hbuf[pl.ds(start, crow), cols] if s == 0 else hs_ref[s - 1, pl.ds(start, crow), :]
                acc = acc + src * wdw_ref[kk:kk + 1, cols]
            cacc_ref[pl.ds(r0, crow), cols] = acc
            return carry

        lax.fori_loop(0, seq // crow, tap_chunk, 0)

    def norm_chunk(c, carry):
        r0 = pl.multiple_of(c * rows, rows)
        hn = _layer_norm(cacc_ref[pl.ds(r0, rows), :], lng_ref[...], lnb_ref[...])
        conv_ref[0, pl.ds(r0, rows), :] = (hn * _sigmoid(hn)).astype(BF16)
        return carry

    lax.fori_loop(0, n_chunks, norm_chunk, 0, unroll=math.gcd(n_chunks, FRONT_UNROLL))


def _front_call(x, fw):
    batch, seq, _ = x.shape
    seq_spec = lambda w: pl.BlockSpec((1, seq, w), lambda b: (b, 0, 0))
    out_shape = (
        jax.ShapeDtypeStruct((batch, seq, ATTN_W), BF16),
        jax.ShapeDtypeStruct((batch, seq, KV_W), BF16),
        jax.ShapeDtypeStruct((batch, seq, 2 * KV_W), BF16),
        jax.ShapeDtypeStruct((batch, seq, POOL_W), BF16),
        jax.ShapeDtypeStruct((batch, seq, CONV_W), BF16),
    )
    consts = (fw["w_front"], fw["cos"], fw["sin_a"], fw["sin_b"], fw["q_gain"], fw["k_gain"],
              fw["seg"], fw["w_mix"], fw["pool_scale"], fw["w_dw"], fw["b_dw"], fw["conv_ln_g"],
              fw["conv_ln_b"])
    return pl.pallas_call(
        _front_kernel,
        grid=(batch,),
        in_specs=[seq_spec(D_MODEL)] + [_const_spec(c.shape) for c in consts],
        out_specs=(seq_spec(ATTN_W), seq_spec(KV_W), seq_spec(2 * KV_W), seq_spec(POOL_W),
                   seq_spec(CONV_W)),
        out_shape=out_shape,
        scratch_shapes=[pltpu.VMEM((seq + 2 * POOL_PAD, POOL_W), F32),
                        pltpu.VMEM((seq + 2 * CONV_PAD, CONV_W), F32),
                        pltpu.VMEM((SUBLANES - 1, seq + 2 * CONV_PAD, LANES), F32),
                        pltpu.VMEM((seq, CONV_W), F32)],
        compiler_params=pltpu.CompilerParams(dimension_semantics=("arbitrary",),
                                             vmem_limit_bytes=VMEM_LIMIT),
        name="front",
    )(x, *consts)


def _attn_kernel(q_ref, k_ref, v_ref, o_ref):
    group = N_HEADS // N_KV_HEADS
    for kv in range(N_KV_HEADS):
        k = k_ref[0, :, kv * HEAD_DIM:(kv + 1) * HEAD_DIM]
        v_ones = v_ref[0, :, 2 * kv * HEAD_DIM:2 * (kv + 1) * HEAD_DIM]
        for g in range(group):
            hcols = slice((kv * group + g) * HEAD_DIM, (kv * group + g + 1) * HEAD_DIM)
            q = q_ref[0, :, hcols]
            s = lax.dot_general(q, k, (((1,), (1,)), ((), ())), preferred_element_type=F32)
            m = jnp.max(s, axis=-1, keepdims=True)
            p = jnp.exp2(s - m).astype(BF16)
            o = jnp.dot(p, v_ones, preferred_element_type=F32)
            o = o / pltpu.roll(o, HEAD_DIM, 1)
            o_ref[0, :, hcols] = o[:, :HEAD_DIM].astype(BF16)


def _attn_call(q, k, v_ones):
    batch, seq, _ = q.shape
    tq = min(ATTN_TQ, seq)
    return pl.pallas_call(
        _attn_kernel,
        grid=(batch, seq // tq),
        in_specs=[pl.BlockSpec((1, tq, ATTN_W), lambda b, i: (b, i, 0)),
                  pl.BlockSpec((1, seq, KV_W), lambda b, i: (b, 0, 0)),
                  pl.BlockSpec((1, seq, 2 * KV_W), lambda b, i: (b, 0, 0))],
        out_specs=pl.BlockSpec((1, tq, ATTN_W), lambda b, i: (b, i, 0)),
        out_shape=jax.ShapeDtypeStruct((batch, seq, ATTN_W), BF16),
        compiler_params=pltpu.CompilerParams(dimension_semantics=("arbitrary", "arbitrary"),
                                             vmem_limit_bytes=VMEM_LIMIT),
        name="attention",
    )(q, k, v_ones)


def _merge_kernel(x_ref, a_ref, p_ref, c_ref, wg_ref, bg_ref, wa_ref, wp_ref, wc_ref, wo_ref,
                  g_ref, b_ref, o_ref):
    sub = min(ROW_GROUP, x_ref.shape[0])
    for r in range(x_ref.shape[0] // sub):
        rows = slice(r * sub, (r + 1) * sub)
        x = x_ref[rows, :]
        xb = x.astype(BF16)
        merged = None
        for br, (in_ref, w_ref) in enumerate(((a_ref, wa_ref), (p_ref, wp_ref), (c_ref, wc_ref))):
            cols = slice(br * D_MODEL, (br + 1) * D_MODEL)
            gate = _sigmoid(jnp.dot(xb, wg_ref[:, cols], preferred_element_type=F32)
                            + bg_ref[:, cols])
            term = gate * jnp.dot(in_ref[rows, :], w_ref[...], preferred_element_type=F32)
            merged = term if merged is None else merged + term
        mix = jnp.dot(merged.astype(BF16), wo_ref[...], preferred_element_type=F32)
        o_ref[rows, :] = _layer_norm(ALPHA * x + mix, g_ref[...], b_ref[...])


def _merge_call(x, attn, pool, conv, mw):
    n = x.shape[0]
    tm = min(TOKEN_TILE, n)
    row_spec = lambda w: pl.BlockSpec((tm, w), lambda i: (i, 0))
    consts = (mw["w_gate"], mw["b_gate"], mw["w_attn_out"], mw["w_pool_out"], mw["w_conv_out"],
              mw["w_out"], mw["ln1_g"], mw["ln1_b"])
    return pl.pallas_call(
        _merge_kernel,
        grid=(n // tm,),
        in_specs=[row_spec(D_MODEL), row_spec(ATTN_W), row_spec(POOL_W), row_spec(CONV_W)]
        + [_const_spec(c.shape) for c in consts],
        out_specs=row_spec(D_MODEL),
        out_shape=jax.ShapeDtypeStruct((n, D_MODEL), F32),
        compiler_params=pltpu.CompilerParams(dimension_semantics=("arbitrary",),
                                             vmem_limit_bytes=VMEM_LIMIT),
        name="merge",
    )(x, attn, pool, conv, *consts)


def _swiglu_partial(xb, wg, wu, wd):
    hg = jnp.dot(xb, wg, preferred_element_type=F32)
    hu = jnp.dot(xb, wu, preferred_element_type=F32)
    h = (hg * _sigmoid(hg) * hu).astype(BF16)
    return jnp.dot(h, wd, preferred_element_type=F32)


def _ffn_kernel(x_ref, wg_ref, wu_ref, wd_ref, g_ref, b_ref, o_ref):
    half = D_FF // 2
    sub = min(ROW_GROUP, x_ref.shape[0])
    for r in range(x_ref.shape[0] // sub):
        rows = slice(r * sub, (r + 1) * sub)
        x = x_ref[rows, :]
        xb = x.astype(BF16)
        f = None
        for c in range(2):
            cols = slice(c * half, (c + 1) * half)
            part = _swiglu_partial(xb, wg_ref[:, cols], wu_ref[:, cols], wd_ref[cols, :])
            f = part if f is None else f + part
        o_ref[rows, :] = _layer_norm(ALPHA * x + f, g_ref[...], b_ref[...])


def _ffn_call(x, dw):
    n = x.shape[0]
    tm = min(TOKEN_TILE, n)
    row_spec = pl.BlockSpec((tm, D_MODEL), lambda i: (i, 0))
    consts = (dw["w_gate"], dw["w_up"], dw["w_down"], dw["ln2_g"], dw["ln2_b"])
    return pl.pallas_call(
        _ffn_kernel,
        grid=(n // tm,),
        in_specs=[row_spec] + [_const_spec(c.shape) for c in consts],
        out_specs=row_spec,
        out_shape=jax.ShapeDtypeStruct((n, D_MODEL), F32),
        compiler_params=pltpu.CompilerParams(dimension_semantics=("arbitrary",),
                                             vmem_limit_bytes=VMEM_LIMIT),
        name="ffn",
    )(x, *consts)


def _route(logits, lane):
    m1 = jnp.max(logits, axis=-1, keepdims=True)
    i1 = jnp.min(jnp.where(logits == m1, lane, LANES), axis=-1, keepdims=True)
    rest = jnp.where(lane == i1, NEG_BIG, logits)
    m2 = jnp.max(rest, axis=-1, keepdims=True)
    i2 = jnp.min(jnp.where(rest == m2, lane, LANES), axis=-1, keepdims=True)
    e2 = jnp.exp(m2 - m1)
    w1 = 1.0 / (1.0 + e2)
    w2 = e2 / (1.0 + e2)
    mask = jnp.where(lane == i1, 1.0, jnp.where(lane == i2, 1.0, 0.0))
    comb = jnp.where(lane == i1, w1, jnp.where(lane == i2, w2, 0.0))
    return mask, comb


def _moe_kernel(x_ref, wr_ref, br_ref, tri_ref, wg_ref, wu_ref, wd_ref, g_ref, b_ref, o_ref,
                comb_ref, rank_ref, rankt_ref, cnt_ref, offs_ref, xy_ref):
    tile = x_ref.shape[0]
    n_sub = tile // MOE_SUB
    e = pl.program_id(1)
    lane = lax.broadcasted_iota(jnp.int32, (MOE_SUB, LANES), 1)
    slot_rows = lax.broadcasted_iota(jnp.int32, (MOE_SLOTS, MOE_SUB), 0).astype(F32)
    slot_cols = lax.broadcasted_iota(jnp.int32, (MOE_SUB, MOE_SUB), 1).astype(F32)

    @pl.when(jnp.logical_and(pl.program_id(0) == 0, e == 0))
    def _():
        xy_ref[...] = jnp.zeros_like(xy_ref)

    @pl.when(e == 0)
    def _():
        for sb in range(n_sub):
            rows = slice(sb * MOE_SUB, (sb + 1) * MOE_SUB)
            xb = x_ref[rows, :].astype(BF16)
            comb_ref[rows, :] = jnp.dot(xb, wr_ref[...], preferred_element_type=F32) + br_ref[...]
        part = min(MOE_ROUTE_ROWS, tile)
        part_lane = lax.broadcasted_iota(jnp.int32, (part, LANES), 1)
        for c in range(tile // part):
            rows = slice(c * part, (c + 1) * part)
            mask, comb = _route(comb_ref[rows, :], part_lane)
            comb_ref[rows, :] = comb
            rank_ref[rows, :] = mask
        for sb in range(n_sub):
            rows = slice(sb * MOE_SUB, (sb + 1) * MOE_SUB)
            mask = rank_ref[rows, :]
            rank = jnp.dot(tri_ref[...], mask.astype(BF16), preferred_element_type=F32)
            rank = jnp.where(mask > 0.0, rank, -1.0)
            rank_ref[rows, :] = rank
            rankt_ref[:, rows] = rank.T
            counts = jnp.sum(mask, axis=0, keepdims=True).astype(jnp.int32)
            for ee in range(N_EXPERTS):
                cnt_ref[sb * N_EXPERTS + ee] = counts[0, ee]
        o_ref[...] = jnp.zeros_like(o_ref)

    def gather(sb, first_slot):
        rows = slice(sb * MOE_SUB, (sb + 1) * MOE_SUB)
        slot_of_token = rankt_ref[pl.ds(e, 1), rows] - first_slot
        take = jnp.where(slot_rows == slot_of_token, 1.0, 0.0).astype(BF16)
        return jnp.dot(take, x_ref[rows, :].astype(BF16), preferred_element_type=F32)

    off = 0
    for sb in range(n_sub):
        xy_ref[pl.ds(pl.multiple_of(off, MOE_ALIGN), MOE_SLOTS), :] = gather(sb, 0.0).astype(BF16)
        offs_ref[sb] = off
        count = cnt_ref[sb * N_EXPERTS + e]
        off = off + ((count + (MOE_ALIGN - 1)) // MOE_ALIGN) * MOE_ALIGN
    n_rows = off

    for sb in range(n_sub):
        count = cnt_ref[sb * N_EXPERTS + e]

        @pl.when(count > MOE_SLOTS)
        def _():
            start = pl.multiple_of(offs_ref[sb] + MOE_SLOTS, MOE_ALIGN)
            row = lax.broadcasted_iota(jnp.int32, (MOE_SLOTS, D_MODEL), 0)
            rest = gather(sb, float(MOE_SLOTS))
            keep = xy_ref[pl.ds(start, MOE_SLOTS), :].astype(F32)
            merged = jnp.where(row < count - MOE_SLOTS, rest, keep)
            xy_ref[pl.ds(start, MOE_SLOTS), :] = merged.astype(BF16)

    def expert_rows(start, size):
        xc = xy_ref[pl.ds(start, size), :]
        y = _swiglu_partial(xc, wg_ref[0], wu_ref[0], wd_ref[0])
        xy_ref[pl.ds(start, size), :] = y.astype(BF16)

    n_full = n_rows // MOE_CHUNK

    def full_body(i, carry):
        expert_rows(pl.multiple_of(i * MOE_CHUNK, MOE_CHUNK), MOE_CHUNK)
        return carry

    lax.fori_loop(0, n_full, full_body, 0)
    n_tail = (n_rows - n_full * MOE_CHUNK + (MOE_TAIL - 1)) // MOE_TAIL

    def tail_body(j, carry):
        expert_rows(pl.multiple_of(n_full * MOE_CHUNK + j * MOE_TAIL, MOE_TAIL), MOE_TAIL)
        return carry

    lax.fori_loop(0, n_tail, tail_body, 0)

    for sb in range(n_sub):
        rows = slice(sb * MOE_SUB, (sb + 1) * MOE_SUB)
        here = lane == e
        slot = jnp.sum(jnp.where(here, rank_ref[rows, :], 0.0), axis=-1, keepdims=True)
        weight = jnp.sum(jnp.where(here, comb_ref[rows, :], 0.0), axis=-1, keepdims=True)
        put = jnp.where(slot_cols == slot, 1.0, 0.0).astype(BF16)
        yb = xy_ref[pl.ds(pl.multiple_of(offs_ref[sb], MOE_ALIGN), MOE_SUB), :]
        o_ref[rows, :] += weight * jnp.dot(put, yb, preferred_element_type=F32)

    @pl.when(e == N_EXPERTS - 1)
    def _():
        o_ref[...] = _layer_norm(ALPHA * x_ref[...] + o_ref[...], g_ref[...], b_ref[...])


def _moe_call(x, ew):
    n = x.shape[0]
    tile = min(MOE_TILE, n)
    n_sub = tile // MOE_SUB
    row_spec = lambda **kw: pl.BlockSpec((tile, D_MODEL), lambda i, e: (i, 0), **kw)
    up_spec = pl.BlockSpec((1, D_MODEL, D_FF_EXPERT), lambda i, e: (e, 0, 0))
    down_spec = pl.BlockSpec((1, D_FF_EXPERT, D_MODEL), lambda i, e: (e, 0, 0))
    consts = (ew["w_router"], ew["b_router"], ew["tri"])
    small = (ew["ln2_g"], ew["ln2_b"])
    return pl.pallas_call(
        _moe_kernel,
        grid=(n // tile, N_EXPERTS),
        in_specs=[row_spec(pipeline_mode=pl.Buffered(1))] + [_const_spec(c.shape) for c in consts]
        + [up_spec, up_spec, down_spec] + [_const_spec(c.shape) for c in small],
        out_specs=row_spec(),
        out_shape=jax.ShapeDtypeStruct((n, D_MODEL), F32),
        scratch_shapes=[pltpu.VMEM((tile, LANES), F32), pltpu.VMEM((tile, LANES), F32),
                        pltpu.VMEM((LANES, tile), F32),
                        pltpu.SMEM((n_sub * N_EXPERTS,), jnp.int32),
                        pltpu.SMEM((n_sub,), jnp.int32),
                        pltpu.VMEM((tile + 2 * MOE_SUB, D_MODEL), BF16)],
        compiler_params=pltpu.CompilerParams(dimension_semantics=("arbitrary", "arbitrary"),
                                             vmem_limit_bytes=VMEM_LIMIT_HIGH),
        name="experts",
    )(x, *consts, ew["w_gate"], ew["w_up"], ew["w_down"], *small)


def _rope_tables(seq):
    rows = seq // GRID_W
    row = jnp.repeat(jnp.arange(rows), GRID_W).astype(F32)
    col = jnp.tile(jnp.arange(GRID_W), rows).astype(F32)
    n_freq = HEAD_DIM // 4
    inv = ROPE_THETA ** (-jnp.arange(n_freq, dtype=F32) / n_freq)
    ang = jnp.concatenate([row[:, None] * inv] * 2 + [col[:, None] * inv] * 2, axis=1)
    ang = jnp.concatenate([ang] * (LANES // HEAD_DIM), axis=1)
    first = (jnp.arange(LANES) % (2 * n_freq)) < n_freq
    cos, sin = jnp.cos(ang), jnp.sin(ang)
    return cos, jnp.where(first, -sin, 0.0), jnp.where(first, 0.0, sin)


def _row(v):
    return v.reshape(1, -1).astype(F32)


def _prepare(p, seq):
    cos, sin_a, sin_b = _rope_tables(seq)
    head = jnp.arange(ATTN_W) // HEAD_DIM
    seg = (head[:, None] == head[None, :]).astype(BF16)
    layers = []
    for l in range(DEPTH):
        w_in = p["w_in"][l]
        front = dict(
            w_front=w_in[:, :FRONT_W].astype(BF16), cos=cos, sin_a=sin_a, sin_b=sin_b,
            q_gain=_row(jnp.tile(p["q_norm_g"][l], N_HEADS)),
            k_gain=_row(jnp.tile(p["k_norm_g"][l], N_KV_HEADS)),
            seg=seg,
            w_mix=jax.scipy.linalg.block_diag(*p["w_pool_mix"][l]).astype(BF16),
            pool_scale=_row(p["pool_scale"][l]), w_dw=p["w_dw"][l].astype(F32),
            b_dw=_row(p["b_dw"][l]), conv_ln_g=_row(p["conv_ln_g"][l]),
            conv_ln_b=_row(p["conv_ln_b"][l]))
        merge = dict(
            w_gate=w_in[:, FRONT_W:].astype(BF16), b_gate=_row(p["b_gate"][l]),
            w_attn_out=p["w_attn_out"][l].astype(BF16), w_pool_out=p["w_pool_out"][l].astype(BF16),
            w_conv_out=p["w_conv_out"][l].astype(BF16), w_out=p["w_out"][l].astype(BF16),
            ln1_g=_row(p["ln1_g"][l]), ln1_b=_row(p["ln1_b"][l]))
        j = l // 2
        if l % 2 == 0:
            mixer = dict(w_gate=p["w_ff_gate"][j].astype(BF16), w_up=p["w_ff_up"][j].astype(BF16),
                         w_down=p["w_ff_down"][j].astype(BF16))
        else:
            pad = LANES - N_EXPERTS
            sub = jnp.arange(MOE_SUB)
            mixer = dict(
                tri=(sub[None, :] < sub[:, None]).astype(BF16),
                w_router=jnp.pad(p["w_router"][j], ((0, 0), (0, pad))).astype(BF16),
                b_router=jnp.pad(_row(p["b_router"][j]), ((0, 0), (0, pad)),
                                 constant_values=NEG_BIG),
                w_gate=p["w_e_gate"][j].astype(BF16), w_up=p["w_e_up"][j].astype(BF16),
                w_down=p["w_e_down"][j].astype(BF16))
        mixer["ln2_g"] = _row(p["ln2_g"][l])
        mixer["ln2_b"] = _row(p["ln2_b"][l])
        layers.append((front, merge, mixer))
    return layers


def _trunk(x, layers):
    batch, seq, _ = x.shape
    for l, (front, merge, mixer) in enumerate(layers):
        q, k, v, pool, conv = _front_call(x, front)
        attn = _attn_call(q, k, v)
        flat = lambda a: a.reshape(batch * seq, a.shape[-1])
        x1 = _merge_call(flat(x), flat(attn), flat(pool), flat(conv), merge)
        x2 = _ffn_call(x1, mixer) if l % 2 == 0 else _moe_call(x1, mixer)
        x = x2.reshape(batch, seq, D_MODEL)
    return x


def kernel(x_prompt, x_sample, w_in, b_gate, q_norm_g, k_norm_g, w_attn_out, w_pool_mix, pool_scale,
           w_pool_out, w_dw, b_dw, conv_ln_g, conv_ln_b, w_conv_out, w_out, ln1_g, ln1_b,
           w_ff_gate, w_ff_up, w_ff_down, w_router, b_router, w_e_gate, w_e_up, w_e_down,
           ln2_g, ln2_b):
    params = dict(w_in=w_in, b_gate=b_gate, q_norm_g=q_norm_g, k_norm_g=k_norm_g,
                  w_attn_out=w_attn_out, w_pool_mix=w_pool_mix, pool_scale=pool_scale,
                  w_pool_out=w_pool_out, w_dw=w_dw, b_dw=b_dw, conv_ln_g=conv_ln_g,
                  conv_ln_b=conv_ln_b, w_conv_out=w_conv_out, w_out=w_out, ln1_g=ln1_g, ln1_b=ln1_b,
                  w_ff_gate=w_ff_gate, w_ff_up=w_ff_up, w_ff_down=w_ff_down, w_router=w_router,
                  b_router=b_router, w_e_gate=w_e_gate, w_e_up=w_e_up, w_e_down=w_e_down,
                  ln2_g=ln2_g, ln2_b=ln2_b)
    assert x_prompt.shape[1] == x_sample.shape[1]
    layers = _prepare(params, x_prompt.shape[1])
    return _trunk(x_prompt, layers), _trunk(x_sample, layers)
```

```python
import math

import jax
import jax.numpy as jnp
from jax import lax
from jax.experimental import pallas as pl
from jax.experimental.pallas import tpu as pltpu

D_MODEL = 1024
DEPTH = 2
GRID_W = 64
N_HEADS = 8
N_KV_HEADS = 2
HEAD_DIM = 64
ATTN_W = N_HEADS * HEAD_DIM
KV_W = N_KV_HEADS * HEAD_DIM
ROPE_THETA = 10000.0
POOL_WINDOWS = (2, 4, 8, 16)
POOL_GROUP = 64
POOL_W = POOL_GROUP * len(POOL_WINDOWS)
CONV_W = 256
CONV_K = 31
N_BRANCH = 3
FRONT_W = ATTN_W + 2 * KV_W + POOL_W + 2 * CONV_W
D_FF = 2816
N_EXPERTS = 8
D_FF_EXPERT = 1408
ALPHA = (2 * DEPTH) ** 0.25
LN_EPS = 1e-5
RMS_EPS = 1e-6

LANES = 128
SUBLANES = 8
POOL_PAD = 8
CONV_PAD = 16
FRONT_ROWS = 256
FRONT_UNROLL = 4
CONV_ROWS = 128
ATTN_TQ = 512
LOG2_E = 1.4426950408889634
TOKEN_TILE = 512
ROW_GROUP = 256
MOE_TILE = 2048
MOE_SUB = 256
MOE_ROUTE_ROWS = 1024
MOE_ALIGN = 16
MOE_SLOTS = MOE_SUB // 2
MOE_CHUNK = 256
MOE_TAIL = 128
NEG_BIG = -1e30
VMEM_LIMIT = 56 * 1024 * 1024
VMEM_LIMIT_HIGH = 60 * 1024 * 1024

F32 = jnp.float32
BF16 = jnp.bfloat16


def _sigmoid(x):
    return 1.0 / (1.0 + jnp.exp(-x))


def _layer_norm(z, g, b):
    mu = jnp.mean(z, axis=-1, keepdims=True)
    d = z - mu
    var = jnp.mean(d * d, axis=-1, keepdims=True)
    return d * lax.rsqrt(var + LN_EPS) * g + b


def _const_spec(shape):
    zeros = (0,) * len(shape)
    return pl.BlockSpec(shape, lambda *_: zeros, pipeline_mode=pl.Buffered(1))


def _segment_sumsq(x, seg):
    sq = x * x
    hi = sq.astype(BF16)
    lo = (sq - hi.astype(F32)).astype(BF16)
    return (jnp.dot(hi, seg, preferred_element_type=F32)
            + jnp.dot(lo, seg, preferred_element_type=F32))


def _norm_rope(x, seg, gain, cos, sin_a, sin_b):
    width = x.shape[-1]
    ms = _segment_sumsq(x, seg) * (1.0 / HEAD_DIM)
    xn = x * lax.rsqrt(ms + RMS_EPS) * gain
    quarter = HEAD_DIM // 4
    up = pltpu.roll(xn, width - quarter, 1)
    down = pltpu.roll(xn, quarter, 1)
    return xn * cos + up * sin_a + down * sin_b


def _front_kernel(x_ref, w_ref, cos_ref, sina_ref, sinb_ref, qg_ref, kg_ref, seg_ref, wmix_ref,
                  pscale_ref, wdw_ref, bdw_ref, lng_ref, lnb_ref,
                  q_ref, k_ref, v_ref, pool_ref, conv_ref, ubuf, hbuf, hs_ref, cacc_ref):
    seq = x_ref.shape[1]
    rows = min(FRONT_ROWS, seq)
    o1, o2, o3 = ATTN_W, ATTN_W + KV_W, ATTN_W + 2 * KV_W
    o4, o5 = o3 + POOL_W, o3 + POOL_W + CONV_W

    ubuf[0:POOL_PAD, :] = jnp.zeros((POOL_PAD, POOL_W), F32)
    ubuf[POOL_PAD + seq:, :] = jnp.zeros((POOL_PAD, POOL_W), F32)
    hbuf[0:CONV_PAD, :] = jnp.zeros((CONV_PAD, CONV_W), F32)
    hbuf[CONV_PAD + seq:, :] = jnp.zeros((CONV_PAD, CONV_W), F32)

    def proj_chunk(c, carry):
        r0 = pl.multiple_of(c * rows, rows)
        xb = x_ref[0, pl.ds(r0, rows), :].astype(BF16)
        proj = jnp.dot(xb, w_ref[...], preferred_element_type=F32)
        cos = cos_ref[pl.ds(r0, rows), :]
        sin_a = sina_ref[pl.ds(r0, rows), :]
        sin_b = sinb_ref[pl.ds(r0, rows), :]
        rep = ATTN_W // LANES
        q = _norm_rope(proj[:, :o1], seg_ref[...], qg_ref[...],
                       jnp.concatenate([cos] * rep, axis=1),
                       jnp.concatenate([sin_a] * rep, axis=1),
                       jnp.concatenate([sin_b] * rep, axis=1))
        q_ref[0, pl.ds(r0, rows), :] = (q * (HEAD_DIM ** -0.5 * LOG2_E)).astype(BF16)
        k = _norm_rope(proj[:, o1:o2], seg_ref[0:KV_W, 0:KV_W], kg_ref[...], cos, sin_a, sin_b)
        k_ref[0, pl.ds(r0, rows), :] = k.astype(BF16)
        v = proj[:, o2:o3]
        low = lax.broadcasted_iota(jnp.int32, v.shape, 1) < HEAD_DIM
        v_ones = [jnp.where(low, v, 1.0), jnp.where(low, pltpu.roll(v, HEAD_DIM, 1), 1.0)]
        v_ref[0, pl.ds(r0, rows), :] = jnp.concatenate(v_ones, axis=1).astype(BF16)
        ubuf[pl.ds(POOL_PAD + r0, rows), :] = proj[:, o3:o4]
        hbuf[pl.ds(CONV_PAD + r0, rows), :] = proj[:, o4:o5] * _sigmoid(proj[:, o5:])
        return carry

    n_chunks = seq // rows
    lax.fori_loop(0, n_chunks, proj_chunk, 0, unroll=math.gcd(n_chunks, FRONT_UNROLL))

    lane = lax.broadcasted_iota(jnp.int32, (rows, LANES), 1)
    first = lane < POOL_GROUP
    for c in range(seq // rows):
        r0 = c * rows
        t = r0 + lax.broadcasted_iota(jnp.int32, (rows, LANES), 0)
        halves = []
        for half in range(POOL_W // LANES):
            w_a, w_b = POOL_WINDOWS[2 * half], POOL_WINDOWS[2 * half + 1]
            lo_a, hi_a = w_a // 2, w_a - w_a // 2 - 1
            lo_b, hi_b = w_b // 2, w_b - w_b // 2 - 1
            cols = slice(half * LANES, (half + 1) * LANES)

            def shifted(off):
                return ubuf[POOL_PAD + r0 + off:POOL_PAD + r0 + off + rows, cols]

            centre = shifted(0)
            tot_a = centre
            for off in range(-lo_a, hi_a + 1):
                if off != 0:
                    tot_a = tot_a + shifted(off)
            tot_b = tot_a
            for off in range(-lo_b, hi_b + 1):
                if off < -lo_a or off > hi_a:
                    tot_b = tot_b + shifted(off)
            lo = jnp.where(first, lo_a, lo_b)
            hi = jnp.where(first, hi_a, hi_b)
            cnt = (jnp.minimum(t + hi + 1, seq) - jnp.maximum(t - lo, 0)).astype(F32)
            tot = jnp.where(first, tot_a, tot_b)
            halves.append(tot / cnt - centre)
        pooled = jnp.concatenate(halves, axis=1).astype(BF16)
        mixed = jnp.dot(pooled, wmix_ref[...], preferred_element_type=F32) * pscale_ref[...]
        pool_ref[0, r0:r0 + rows, :] = mixed.astype(BF16)

    crow = min(CONV_ROWS, seq)
    shift0 = CONV_PAD - CONV_K // 2
    tail_rows = 2 * CONV_PAD - SUBLANES
    n_half = CONV_W // LANES
    for half in range(n_half):
        cols = slice(half * LANES, (half + 1) * LANES)

        def shift_chunk(c, carry):
            j0 = pl.multiple_of(c * crow, crow)
            win = hbuf[pl.ds(j0, crow + SUBLANES), cols]
            for s in range(1, SUBLANES):
                hs_ref[s - 1, pl.ds(j0, crow), :] = win[s:s + crow, :]
            return carry

        lax.fori_loop(0, seq // crow, shift_chunk, 0)
        tail = hbuf[seq:seq + 2 * CONV_PAD, cols]
        for s in range(1, SUBLANES):
            hs_ref[s - 1, seq:seq + tail_rows, :] = tail[s:s + tail_rows, :]

        def tap_chunk(c, carry):
            r0 = pl.multiple_of(c * crow, crow)
            acc = jnp.broadcast_to(bdw_ref[:, cols], (crow, LANES))
            for kk in range(CONV_K):
                a, s = divmod(kk + shift0, SUBLANES)
                start = r0 + SUBLANES * a
                src = hbuf[pl.ds(start, crow), cols] if s == 0 else hs_ref[s - 1, pl.ds(start, crow), :]
                acc = acc + src * wdw_ref[kk:kk + 1, cols]
            cacc_ref[pl.ds(r0, crow), cols] = acc
            return carry

        lax.fori_loop(0, seq // crow, tap_chunk, 0)

    def norm_chunk(c, carry):
        r0 = pl.multiple_of(c * rows, rows)
        hn = _layer_norm(cacc_ref[pl.ds(r0, rows), :], lng_ref[...], lnb_ref[...])
        conv_ref[0, pl.ds(r0, rows), :] = (hn * _sigmoid(hn)).astype(BF16)
        return carry

    lax.fori_loop(0, n_chunks, norm_chunk, 0, unroll=math.gcd(n_chunks, FRONT_UNROLL))


def _front_call(x, fw):
    batch, seq, _ = x.shape
    seq_spec = lambda w: pl.BlockSpec((1, seq, w), lambda b: (b, 0, 0))
    out_shape = (
        jax.ShapeDtypeStruct((batch, seq, ATTN_W), BF16),
        jax.ShapeDtypeStruct((batch, seq, KV_W), BF16),
        jax.ShapeDtypeStruct((batch, seq, 2 * KV_W), BF16),
        jax.ShapeDtypeStruct((batch, seq, POOL_W), BF16),
        jax.ShapeDtypeStruct((batch, seq, CONV_W), BF16),
    )
    consts = (fw["w_front"], fw["cos"], fw["sin_a"], fw["sin_b"], fw["q_gain"], fw["k_gain"],
              fw["seg"], fw["w_mix"], fw["pool_scale"], fw["w_dw"], fw["b_dw"], fw["conv_ln_g"],
              fw["conv_ln_b"])
    return pl.pallas_call(
        _front_kernel,
        grid=(batch,),
        in_specs=[seq_spec(D_MODEL)] + [_const_spec(c.shape) for c in consts],
        out_specs=(seq_spec(ATTN_W), seq_spec(KV_W), seq_spec(2 * KV_W), seq_spec(POOL_W),
                   seq_spec(CONV_W)),
        out_shape=out_shape,
        scratch_shapes=[pltpu.VMEM((seq + 2 * POOL_PAD, POOL_W), F32),
                        pltpu.VMEM((seq + 2 * CONV_PAD, CONV_W), F32),
                        pltpu.VMEM((SUBLANES - 1, seq + 2 * CONV_PAD, LANES), F32),
                        pltpu.VMEM((seq, CONV_W), F32)],
        compiler_params=pltpu.CompilerParams(dimension_semantics=("arbitrary",),
                                             vmem_limit_bytes=VMEM_LIMIT),
        name="front",
    )(x, *consts)


def _attn_kernel(q_ref, k_ref, v_ref, o_ref):
    group = N_HEADS // N_KV_HEADS

    def scores(head):
        kv = head // group
        q = q_ref[0, :, head * HEAD_DIM:(head + 1) * HEAD_DIM]
        k = k_ref[0, :, kv * HEAD_DIM:(kv + 1) * HEAD_DIM]
        return lax.dot_general(q, k, (((1,), (1,)), ((), ())), preferred_element_type=F32)

    s_next = scores(0)
    for head in range(N_HEADS):
        s = s_next
        if head + 1 < N_HEADS:
            s_next = scores(head + 1)
        kv = head // group
        v_ones = v_ref[0, :, 2 * kv * HEAD_DIM:2 * (kv + 1) * HEAD_DIM]
        m = jnp.max(s, axis=-1, keepdims=True)
        p = jnp.exp2(s - m).astype(BF16)
        o = jnp.dot(p, v_ones, preferred_element_type=F32)
        o = o / pltpu.roll(o, HEAD_DIM, 1)
        o_ref[0, :, head * HEAD_DIM:(head + 1) * HEAD_DIM] = o[:, :HEAD_DIM].astype(BF16)


def _attn_call(q, k, v_ones):
    batch, seq, _ = q.shape
    tq = min(ATTN_TQ, seq)
    return pl.pallas_call(
        _attn_kernel,
        grid=(batch, seq // tq),
        in_specs=[pl.BlockSpec((1, tq, ATTN_W), lambda b, i: (b, i, 0)),
                  pl.BlockSpec((1, seq, KV_W), lambda b, i: (b, 0, 0)),
                  pl.BlockSpec((1, seq, 2 * KV_W), lambda b, i: (b, 0, 0))],
        out_specs=pl.BlockSpec((1, tq, ATTN_W), lambda b, i: (b, i, 0)),
        out_shape=jax.ShapeDtypeStruct((batch, seq, ATTN_W), BF16),
        compiler_params=pltpu.CompilerParams(dimension_semantics=("arbitrary", "arbitrary"),
                                             vmem_limit_bytes=VMEM_LIMIT),
        name="attention",
    )(q, k, v_ones)


def _merge_kernel(x_ref, a_ref, p_ref, c_ref, wg_ref, bg_ref, wa_ref, wp_ref, wc_ref, wo_ref,
                  g_ref, b_ref, o_ref):
    sub = min(ROW_GROUP, x_ref.shape[0])
    for r in range(x_ref.shape[0] // sub):
        rows = slice(r * sub, (r + 1) * sub)
        x = x_ref[rows, :]
        xb = x.astype(BF16)
        merged = None
        for br, (in_ref, w_ref) in enumerate(((a_ref, wa_ref), (p_ref, wp_ref), (c_ref, wc_ref))):
            cols = slice(br * D_MODEL, (br + 1) * D_MODEL)
            gate = _sigmoid(jnp.dot(xb, wg_ref[:, cols], preferred_element_type=F32)
                            + bg_ref[:, cols])
            term = gate * jnp.dot(in_ref[rows, :], w_ref[...], preferred_element_type=F32)
            merged = term if merged is None else merged + term
        mix = jnp.dot(merged.astype(BF16), wo_ref[...], preferred_element_type=F32)
        o_ref[rows, :] = _layer_norm(ALPHA * x + mix, g_ref[...], b_ref[...])


def _merge_call(x, attn, pool, conv, mw):
    n = x.shape[0]
    tm = min(TOKEN_TILE, n)
    row_spec = lambda w: pl.BlockSpec((tm, w), lambda i: (i, 0))
    consts = (mw["w_gate"], mw["b_gate"], mw["w_attn_out"], mw["w_pool_out"], mw["w_conv_out"],
              mw["w_out"], mw["ln1_g"], mw["ln1_b"])
    return pl.pallas_call(
        _merge_kernel,
        grid=(n // tm,),
        in_specs=[row_spec(D_MODEL), row_spec(ATTN_W), row_spec(POOL_W), row_spec(CONV_W)]
        + [_const_spec(c.shape) for c in consts],
        out_specs=row_spec(D_MODEL),
        out_shape=jax.ShapeDtypeStruct((n, D_MODEL), F32),
        compiler_params=pltpu.CompilerParams(dimension_semantics=("arbitrary",),
                                             vmem_limit_bytes=VMEM_LIMIT),
        name="merge",
    )(x, attn, pool, conv, *consts)


def _swiglu_partial(xb, wg, wu, wd):
    hg = jnp.dot(xb, wg, preferred_element_type=F32)
    hu = jnp.dot(xb, wu, preferred_element_type=F32)
    h = (hg * _sigmoid(hg) * hu).astype(BF16)
    return jnp.dot(h, wd, preferred_element_type=F32)


def _ffn_kernel(x_ref, wg_ref, wu_ref, wd_ref, g_ref, b_ref, o_ref):
    half = D_FF // 2
    sub = min(ROW_GROUP, x_ref.shape[0])
    for r in range(x_ref.shape[0] // sub):
        rows = slice(r * sub, (r + 1) * sub)
        x = x_ref[rows, :]
        xb = x.astype(BF16)
        f = None
        for c in range(2):
            cols = slice(c * half, (c + 1) * half)
            part = _swiglu_partial(xb, wg_ref[:, cols], wu_ref[:, cols], wd_ref[cols, :])
            f = part if f is None else f + part
        o_ref[rows, :] = _layer_norm(ALPHA * x + f, g_ref[...], b_ref[...])


def _ffn_call(x, dw):
    n = x.shape[0]
    tm = min(TOKEN_TILE, n)
    row_spec = pl.BlockSpec((tm, D_MODEL), lambda i: (i, 0))
    consts = (dw["w_gate"], dw["w_up"], dw["w_down"], dw["ln2_g"], dw["ln2_b"])
    return pl.pallas_call(
        _ffn_kernel,
        grid=(n // tm,),
        in_specs=[row_spec] + [_const_spec(c.shape) for c in consts],
        out_specs=row_spec,
        out_shape=jax.ShapeDtypeStruct((n, D_MODEL), F32),
        compiler_params=pltpu.CompilerParams(dimension_semantics=("arbitrary",),
                                             vmem_limit_bytes=VMEM_LIMIT),
        name="ffn",
    )(x, *consts)


def _route(logits, lane):
    m1 = jnp.max(logits, axis=-1, keepdims=True)
    i1 = jnp.min(jnp.where(logits == m1, lane, LANES), axis=-1, keepdims=True)
    rest = jnp.where(lane == i1, NEG_BIG, logits)
    m2 = jnp.max(rest, axis=-1, keepdims=True)
    i2 = jnp.min(jnp.where(rest == m2, lane, LANES), axis=-1, keepdims=True)
    e2 = jnp.exp(m2 - m1)
    w1 = 1.0 / (1.0 + e2)
    w2 = e2 / (1.0 + e2)
    mask = jnp.where(lane == i1, 1.0, jnp.where(lane == i2, 1.0, 0.0))
    comb = jnp.where(lane == i1, w1, jnp.where(lane == i2, w2, 0.0))
    return mask, comb


def _moe_kernel(x_ref, wr_ref, br_ref, tri_ref, wg_ref, wu_ref, wd_ref, g_ref, b_ref, o_ref,
                comb_ref, rank_ref, rankt_ref, cnt_ref, offs_ref, xy_ref):
    tile = x_ref.shape[0]
    n_sub = tile // MOE_SUB
    e = pl.program_id(1)
    lane = lax.broadcasted_iota(jnp.int32, (MOE_SUB, LANES), 1)
    slot_rows = lax.broadcasted_iota(jnp.int32, (MOE_SLOTS, MOE_SUB), 0).astype(F32)
    slot_cols = lax.broadcasted_iota(jnp.int32, (MOE_SUB, MOE_SUB), 1).astype(F32)

    @pl.when(jnp.logical_and(pl.program_id(0) == 0, e == 0))
    def _():
        xy_ref[...] = jnp.zeros_like(xy_ref)

    @pl.when(e == 0)
    def _():
        for sb in range(n_sub):
            rows = slice(sb * MOE_SUB, (sb + 1) * MOE_SUB)
            xb = x_ref[rows, :].astype(BF16)
            comb_ref[rows, :] = jnp.dot(xb, wr_ref[...], preferred_element_type=F32) + br_ref[...]
        part = min(MOE_ROUTE_ROWS, tile)
        part_lane = lax.broadcasted_iota(jnp.int32, (part, LANES), 1)
        for c in range(tile // part):
            rows = slice(c * part, (c + 1) * part)
            mask, comb = _route(comb_ref[rows, :], part_lane)
            comb_ref[rows, :] = comb
            rank_ref[rows, :] = mask
        for sb in range(n_sub):
            rows = slice(sb * MOE_SUB, (sb + 1) * MOE_SUB)
            mask = rank_ref[rows, :]
            rank = jnp.dot(tri_ref[...], mask.astype(BF16), preferred_element_type=F32)
            rank = jnp.where(mask > 0.0, rank, -1.0)
            rank_ref[rows, :] = rank
            rankt_ref[:, rows] = rank.T
            counts = jnp.sum(mask, axis=0, keepdims=True).astype(jnp.int32)
            for ee in range(N_EXPERTS):
                cnt_ref[sb * N_EXPERTS + ee] = counts[0, ee]
        o_ref[...] = jnp.zeros_like(o_ref)

    def gather(sb, first_slot):
        rows = slice(sb * MOE_SUB, (sb + 1) * MOE_SUB)
        slot_of_token = rankt_ref[pl.ds(e, 1), rows] - first_slot
        take = jnp.where(slot_rows == slot_of_token, 1.0, 0.0).astype(BF16)
        return jnp.dot(take, x_ref[rows, :].astype(BF16), preferred_element_type=F32)

    off = 0
    for sb in range(n_sub):
        xy_ref[pl.ds(pl.multiple_of(off, MOE_ALIGN), MOE_SLOTS), :] = gather(sb, 0.0).astype(BF16)
        offs_ref[sb] = off
        count = cnt_ref[sb * N_EXPERTS + e]
        off = off + ((count + (MOE_ALIGN - 1)) // MOE_ALIGN) * MOE_ALIGN
    n_rows = off

    for sb in range(n_sub):
        count = cnt_ref[sb * N_EXPERTS + e]

        @pl.when(count > MOE_SLOTS)
        def _():
            start = pl.multiple_of(offs_ref[sb] + MOE_SLOTS, MOE_ALIGN)
            row = lax.broadcasted_iota(jnp.int32, (MOE_SLOTS, D_MODEL), 0)
            rest = gather(sb, float(MOE_SLOTS))
            keep = xy_ref[pl.ds(start, MOE_SLOTS), :].astype(F32)
            merged = jnp.where(row < count - MOE_SLOTS, rest, keep)
            xy_ref[pl.ds(start, MOE_SLOTS), :] = merged.astype(BF16)

    def expert_rows(start, size):
        xc = xy_ref[pl.ds(start, size), :]
        y = _swiglu_partial(xc, wg_ref[0], wu_ref[0], wd_ref[0])
        xy_ref[pl.ds(start, size), :] = y.astype(BF16)

    n_full = n_rows // MOE_CHUNK

    def full_body(i, carry):
        expert_rows(pl.multiple_of(i * MOE_CHUNK, MOE_CHUNK), MOE_CHUNK)
        return carry

    lax.fori_loop(0, n_full, full_body, 0)
    n_tail = (n_rows - n_full * MOE_CHUNK + (MOE_TAIL - 1)) // MOE_TAIL

    def tail_body(j, carry):
        expert_rows(pl.multiple_of(n_full * MOE_CHUNK + j * MOE_TAIL, MOE_TAIL), MOE_TAIL)
        return carry

    lax.fori_loop(0, n_tail, tail_body, 0)

    for sb in range(n_sub):
        rows = slice(sb * MOE_SUB, (sb + 1) * MOE_SUB)
        here = lane == e
        slot = jnp.sum(jnp.where(here, rank_ref[rows, :], 0.0), axis=-1, keepdims=True)
        weight = jnp.sum(jnp.where(here, comb_ref[rows, :], 0.0), axis=-1, keepdims=True)
        put = jnp.where(slot_cols == slot, 1.0, 0.0).astype(BF16)
        yb = xy_ref[pl.ds(pl.multiple_of(offs_ref[sb], MOE_ALIGN), MOE_SUB), :]
        o_ref[rows, :] += weight * jnp.dot(put, yb, preferred_element_type=F32)

    @pl.when(e == N_EXPERTS - 1)
    def _():
        o_ref[...] = _layer_norm(ALPHA * x_ref[...] + o_ref[...], g_ref[...], b_ref[...])


def _moe_call(x, ew):
    n = x.shape[0]
    tile = min(MOE_TILE, n)
    n_sub = tile // MOE_SUB
    row_spec = lambda **kw: pl.BlockSpec((tile, D_MODEL), lambda i, e: (i, 0), **kw)
    up_spec = pl.BlockSpec((1, D_MODEL, D_FF_EXPERT), lambda i, e: (e, 0, 0))
    down_spec = pl.BlockSpec((1, D_FF_EXPERT, D_MODEL), lambda i, e: (e, 0, 0))
    consts = (ew["w_router"], ew["b_router"], ew["tri"])
    small = (ew["ln2_g"], ew["ln2_b"])
    return pl.pallas_call(
        _moe_kernel,
        grid=(n // tile, N_EXPERTS),
        in_specs=[row_spec(pipeline_mode=pl.Buffered(1))] + [_const_spec(c.shape) for c in consts]
        + [up_spec, up_spec, down_spec] + [_const_spec(c.shape) for c in small],
        out_specs=row_spec(),
        out_shape=jax.ShapeDtypeStruct((n, D_MODEL), F32),
        scratch_shapes=[pltpu.VMEM((tile, LANES), F32), pltpu.VMEM((tile, LANES), F32),
                        pltpu.VMEM((LANES, tile), F32),
                        pltpu.SMEM((n_sub * N_EXPERTS,), jnp.int32),
                        pltpu.SMEM((n_sub,), jnp.int32),
                        pltpu.VMEM((tile + 2 * MOE_SUB, D_MODEL), BF16)],
        compiler_params=pltpu.CompilerParams(dimension_semantics=("arbitrary", "arbitrary"),
                                             vmem_limit_bytes=VMEM_LIMIT_HIGH),
        name="experts",
    )(x, *consts, ew["w_gate"], ew["w_up"], ew["w_down"], *small)


def _rope_tables(seq):
    rows = seq // GRID_W
    row = jnp.repeat(jnp.arange(rows), GRID_W).astype(F32)
    col = jnp.tile(jnp.arange(GRID_W), rows).astype(F32)
    n_freq = HEAD_DIM // 4
    inv = ROPE_THETA ** (-jnp.arange(n_freq, dtype=F32) / n_freq)
    ang = jnp.concatenate([row[:, None] * inv] * 2 + [col[:, None] * inv] * 2, axis=1)
    ang = jnp.concatenate([ang] * (LANES // HEAD_DIM), axis=1)
    first = (jnp.arange(LANES) % (2 * n_freq)) < n_freq
    cos, sin = jnp.cos(ang), jnp.sin(ang)
    return cos, jnp.where(first, -sin, 0.0), jnp.where(first, 0.0, sin)


def _row(v):
    return v.reshape(1, -1).astype(F32)


def _prepare(p, seq):
    cos, sin_a, sin_b = _rope_tables(seq)
    head = jnp.arange(ATTN_W) // HEAD_DIM
    seg = (head[:, None] == head[None, :]).astype(BF16)
    layers = []
    for l in range(DEPTH):
        w_in = p["w_in"][l]
        front = dict(
            w_front=w_in[:, :FRONT_W].astype(BF16), cos=cos, sin_a=sin_a, sin_b=sin_b,
            q_gain=_row(jnp.tile(p["q_norm_g"][l], N_HEADS)),
            k_gain=_row(jnp.tile(p["k_norm_g"][l], N_KV_HEADS)),
            seg=seg,
            w_mix=jax.scipy.linalg.block_diag(*p["w_pool_mix"][l]).astype(BF16),
            pool_scale=_row(p["pool_scale"][l]), w_dw=p["w_dw"][l].astype(F32),
            b_dw=_row(p["b_dw"][l]), conv_ln_g=_row(p["conv_ln_g"][l]),
            conv_ln_b=_row(p["conv_ln_b"][l]))
        merge = dict(
            w_gate=w_in[:, FRONT_W:].astype(BF16), b_gate=_row(p["b_gate"][l]),
            w_attn_out=p["w_attn_out"][l].astype(BF16), w_pool_out=p["w_pool_out"][l].astype(BF16),
            w_conv_out=p["w_conv_out"][l].astype(BF16), w_out=p["w_out"][l].astype(BF16),
            ln1_g=_row(p["ln1_g"][l]), ln1_b=_row(p["ln1_b"][l]))
        j = l // 2
        if l % 2 == 0:
            mixer = dict(w_gate=p["w_ff_gate"][j].astype(BF16), w_up=p["w_ff_up"][j].astype(BF16),
                         w_down=p["w_ff_down"][j].astype(BF16))
        else:
            pad = LANES - N_EXPERTS
            sub = jnp.arange(MOE_SUB)
            mixer = dict(
                tri=(sub[None, :] < sub[:, None]).astype(BF16),
                w_router=jnp.pad(p["w_router"][j], ((0, 0), (0, pad))).astype(BF16),
                b_router=jnp.pad(_row(p["b_router"][j]), ((0, 0), (0, pad)),
                                 constant_values=NEG_BIG),
                w_gate=p["w_e_gate"][j].astype(BF16), w_up=p["w_e_up"][j].astype(BF16),
                w_down=p["w_e_down"][j].astype(BF16))
        mixer["ln2_g"] = _row(p["ln2_g"][l])
        mixer["ln2_b"] = _row(p["ln2_b"][l])
        layers.append((front, merge, mixer))
    return layers


def _trunk(x, layers):
    batch, seq, _ = x.shape
    for l, (front, merge, mixer) in enumerate(layers):
        q, k, v, pool, conv = _front_call(x, front)
        attn = _attn_call(q, k, v)
        flat = lambda a: a.reshape(batch * seq, a.shape[-1])
        x1 = _merge_call(flat(x), flat(attn), flat(pool), flat(conv), merge)
        x2 = _ffn_call(x1, mixer) if l % 2 == 0 else _moe_call(x1, mixer)
        x = x2.reshape(batch, seq, D_MODEL)
    return x


def kernel(x_prompt, x_sample, w_in, b_gate, q_norm_g, k_norm_g, w_attn_out, w_pool_mix, pool_scale,
           w_pool_out, w_dw, b_dw, conv_ln_g, conv_ln_b, w_conv_out, w_out, ln1_g, ln1_b,
           w_ff_gate, w_ff_up, w_ff_down, w_router, b_router, w_e_gate, w_e_up, w_e_down,
           ln2_g, ln2_b):
    params = dict(w_in=w_in, b_gate=b_gate, q_norm_g=q_norm_g, k_norm_g=k_norm_g,
                  w_attn_out=w_attn_out, w_pool_mix=w_pool_mix, pool_scale=pool_scale,
                  w_pool_out=w_pool_out, w_dw=w_dw, b_dw=b_dw, conv_ln_g=conv_ln_g,
                  conv_ln_b=conv_ln_b, w_conv_out=w_conv_out, w_out=w_out, ln1_g=ln1_g, ln1_b=ln1_b,
                  w_ff_gate=w_ff_gate, w_ff_up=w_ff_up, w_ff_down=w_ff_down, w_router=w_router,
                  b_router=b_router, w_e_gate=w_e_gate, w_e_up=w_e_up, w_e_down=w_e_down,
                  ln2_g=ln2_g, ln2_b=ln2_b)
    assert x_prompt.shape[1] == x_sample.shape[1]
    layers = _prepare(params, x_prompt.shape[1])
    return _trunk(x_prompt, layers), _trunk(x_sample, layers)
```

```python
import math

import jax
import jax.numpy as jnp
from jax import lax
from jax.experimental import pallas as pl
from jax.experimental.pallas import tpu as pltpu

D_MODEL = 1024
DEPTH = 2
GRID_W = 64
N_HEADS = 8
N_KV_HEADS = 2
HEAD_DIM = 64
ATTN_W = N_HEADS * HEAD_DIM
KV_W = N_KV_HEADS * HEAD_DIM
ROPE_THETA = 10000.0
POOL_WINDOWS = (2, 4, 8, 16)
POOL_GROUP = 64
POOL_W = POOL_GROUP * len(POOL_WINDOWS)
CONV_W = 256
CONV_K = 31
N_BRANCH = 3
FRONT_W = ATTN_W + 2 * KV_W + POOL_W + 2 * CONV_W
D_FF = 2816
N_EXPERTS = 8
D_FF_EXPERT = 1408
ALPHA = (2 * DEPTH) ** 0.25
LN_EPS = 1e-5
RMS_EPS = 1e-6

LANES = 128
SUBLANES = 8
POOL_PAD = 8
CONV_PAD = 16
FRONT_ROWS = 256
FRONT_UNROLL = 8
CONV_ROWS = 128
ATTN_TQ = 512
LOG2_E = 1.4426950408889634
TOKEN_TILE = 1024
ROW_GROUP = 256
MOE_TILE = 2048
MOE_SUB = 256
MOE_ROUTE_ROWS = 1024
MOE_ALIGN = 16
MOE_SLOTS = MOE_SUB // 2
MOE_CHUNK = 256
MOE_TAIL = 128
NEG_BIG = -1e30
VMEM_LIMIT = 56 * 1024 * 1024
VMEM_LIMIT_HIGH = 60 * 1024 * 1024

F32 = jnp.float32
BF16 = jnp.bfloat16


def _sigmoid(x):
    return 1.0 / (1.0 + jnp.exp(-x))


def _layer_norm(z, g, b):
    mu = jnp.mean(z, axis=-1, keepdims=True)
    d = z - mu
    var = jnp.mean(d * d, axis=-1, keepdims=True)
    return d * lax.rsqrt(var + LN_EPS) * g + b


def _const_spec(shape):
    zeros = (0,) * len(shape)
    return pl.BlockSpec(shape, lambda *_: zeros, pipeline_mode=pl.Buffered(1))


def _segment_sumsq(x, seg):
    sq = x * x
    hi = sq.astype(BF16)
    lo = (sq - hi.astype(F32)).astype(BF16)
    return (jnp.dot(hi, seg, preferred_element_type=F32)
            + jnp.dot(lo, seg, preferred_element_type=F32))


def _norm_rope(x, seg, gain, cos, sin_a, sin_b):
    width = x.shape[-1]
    ms = _segment_sumsq(x, seg) * (1.0 / HEAD_DIM)
    xn = x * lax.rsqrt(ms + RMS_EPS) * gain
    quarter = HEAD_DIM // 4
    up = pltpu.roll(xn, width - quarter, 1)
    down = pltpu.roll(xn, quarter, 1)
    return xn * cos + up * sin_a + down * sin_b


def _front_kernel(x_ref, w_ref, cos_ref, sina_ref, sinb_ref, qg_ref, kg_ref, seg_ref, wmix_ref,
                  pscale_ref, wdw_ref, bdw_ref, lng_ref, lnb_ref,
                  q_ref, k_ref, v_ref, pool_ref, conv_ref, ubuf, hbuf, hs_ref, cacc_ref):
    seq = x_ref.shape[1]
    rows = min(FRONT_ROWS, seq)
    o1, o2, o3 = ATTN_W, ATTN_W + KV_W, ATTN_W + 2 * KV_W
    o4, o5 = o3 + POOL_W, o3 + POOL_W + CONV_W

    ubuf[0:POOL_PAD, :] = jnp.zeros((POOL_PAD, POOL_W), F32)
    ubuf[POOL_PAD + seq:, :] = jnp.zeros((POOL_PAD, POOL_W), F32)
    hbuf[0:CONV_PAD, :] = jnp.zeros((CONV_PAD, CONV_W), F32)
    hbuf[CONV_PAD + seq:, :] = jnp.zeros((CONV_PAD, CONV_W), F32)

    def proj_chunk(c, carry):
        r0 = pl.multiple_of(c * rows, rows)
        xb = x_ref[0, pl.ds(r0, rows), :].astype(BF16)
        proj = jnp.dot(xb, w_ref[...], preferred_element_type=F32)
        cos = cos_ref[pl.ds(r0, rows), :]
        sin_a = sina_ref[pl.ds(r0, rows), :]
        sin_b = sinb_ref[pl.ds(r0, rows), :]
        rep = ATTN_W // LANES
        q = _norm_rope(proj[:, :o1], seg_ref[...], qg_ref[...],
                       jnp.concatenate([cos] * rep, axis=1),
                       jnp.concatenate([sin_a] * rep, axis=1),
                       jnp.concatenate([sin_b] * rep, axis=1))
        q_ref[0, pl.ds(r0, rows), :] = (q * (HEAD_DIM ** -0.5 * LOG2_E)).astype(BF16)
        k = _norm_rope(proj[:, o1:o2], seg_ref[0:KV_W, 0:KV_W], kg_ref[...], cos, sin_a, sin_b)
        k_ref[0, pl.ds(r0, rows), :] = k.astype(BF16)
        v = proj[:, o2:o3]
        low = lax.broadcasted_iota(jnp.int32, v.shape, 1) < HEAD_DIM
        v_ones = [jnp.where(low, v, 1.0), jnp.where(low, pltpu.roll(v, HEAD_DIM, 1), 1.0)]
        v_ref[0, pl.ds(r0, rows), :] = jnp.concatenate(v_ones, axis=1).astype(BF16)
        ubuf[pl.ds(POOL_PAD + r0, rows), :] = proj[:, o3:o4]
        hbuf[pl.ds(CONV_PAD + r0, rows), :] = proj[:, o4:o5] * _sigmoid(proj[:, o5:])
        return carry

    n_chunks = seq // rows
    lax.fori_loop(0, n_chunks, proj_chunk, 0, unroll=math.gcd(n_chunks, FRONT_UNROLL))

    lane = lax.broadcasted_iota(jnp.int32, (rows, LANES), 1)
    first = lane < POOL_GROUP
    for c in range(seq // rows):
        r0 = c * rows
        t = r0 + lax.broadcasted_iota(jnp.int32, (rows, LANES), 0)
        halves = []
        for half in range(POOL_W // LANES):
            w_a, w_b = POOL_WINDOWS[2 * half], POOL_WINDOWS[2 * half + 1]
            lo_a, hi_a = w_a // 2, w_a - w_a // 2 - 1
            lo_b, hi_b = w_b // 2, w_b - w_b // 2 - 1
            cols = slice(half * LANES, (half + 1) * LANES)

            def shifted(off):
                return ubuf[POOL_PAD + r0 + off:POOL_PAD + r0 + off + rows, cols]

            centre = shifted(0)
            tot_a = centre
            for off in range(-lo_a, hi_a + 1):
                if off != 0:
                    tot_a = tot_a + shifted(off)
            tot_b = tot_a
            for off in range(-lo_b, hi_b + 1):
                if off < -lo_a or off > hi_a:
                    tot_b = tot_b + shifted(off)
            lo = jnp.where(first, lo_a, lo_b)
            hi = jnp.where(first, hi_a, hi_b)
            cnt = (jnp.minimum(t + hi + 1, seq) - jnp.maximum(t - lo, 0)).astype(F32)
            tot = jnp.where(first, tot_a, tot_b)
            halves.append(tot / cnt - centre)
        pooled = jnp.concatenate(halves, axis=1).astype(BF16)
        mixed = jnp.dot(pooled, wmix_ref[...], preferred_element_type=F32) * pscale_ref[...]
        pool_ref[0, r0:r0 + rows, :] = mixed.astype(BF16)

    crow = min(CONV_ROWS, seq)
    shift0 = CONV_PAD - CONV_K // 2
    tail_rows = 2 * CONV_PAD - SUBLANES
    n_half = CONV_W // LANES
    for half in range(n_half):
        cols = slice(half * LANES, (half + 1) * LANES)

        def shift_chunk(c, carry):
            j0 = pl.multiple_of(c * crow, crow)
            win = hbuf[pl.ds(j0, crow + SUBLANES), cols]
            for s in range(1, SUBLANES):
                hs_ref[s - 1, pl.ds(j0, crow), :] = win[s:s + crow, :]
            return carry

        lax.fori_loop(0, seq // crow, shift_chunk, 0)
        tail = hbuf[seq:seq + 2 * CONV_PAD, cols]
        for s in range(1, SUBLANES):
            hs_ref[s - 1, seq:seq + tail_rows, :] = tail[s:s + tail_rows, :]

        def tap_chunk(c, carry):
            r0 = pl.multiple_of(c * crow, crow)
            acc = jnp.broadcast_to(bdw_ref[:, cols], (crow, LANES))
            for kk in range(CONV_K):
                a, s = divmod(kk + shift0, SUBLANES)
                start = r0 + SUBLANES * a
                src = hbuf[pl.ds(start, crow), cols] if s == 0 else hs_ref[s - 1, pl.ds(start, crow), :]
                acc = acc + src * wdw_ref[kk:kk + 1, cols]
            cacc_ref[pl.ds(r0, crow), cols] = acc
            return carry

        lax.fori_loop(0, seq // crow, tap_chunk, 0)

    def norm_chunk(c, carry):
        r0 = pl.multiple_of(c * rows, rows)
        hn = _layer_norm(cacc_ref[pl.ds(r0, rows), :], lng_ref[...], lnb_ref[...])
        conv_ref[0, pl.ds(r0, rows), :] = (hn * _sigmoid(hn)).astype(BF16)
        return carry

    lax.fori_loop(0, n_chunks, norm_chunk, 0, unroll=math.gcd(n_chunks, FRONT_UNROLL))


def _front_call(x, fw):
    batch, seq, _ = x.shape
    seq_spec = lambda w: pl.BlockSpec((1, seq, w), lambda b: (b, 0, 0))
    out_shape = (
        jax.ShapeDtypeStruct((batch, seq, ATTN_W), BF16),
        jax.ShapeDtypeStruct((batch, seq, KV_W), BF16),
        jax.ShapeDtypeStruct((batch, seq, 2 * KV_W), BF16),
        jax.ShapeDtypeStruct((batch, seq, POOL_W), BF16),
        jax.ShapeDtypeStruct((batch, seq, CONV_W), BF16),
    )
    consts = (fw["w_front"], fw["cos"], fw["sin_a"], fw["sin_b"], fw["q_gain"], fw["k_gain"],
              fw["seg"], fw["w_mix"], fw["pool_scale"], fw["w_dw"], fw["b_dw"], fw["conv_ln_g"],
              fw["conv_ln_b"])
    return pl.pallas_call(
        _front_kernel,
        grid=(batch,),
        in_specs=[seq_spec(D_MODEL)] + [_const_spec(c.shape) for c in consts],
        out_specs=(seq_spec(ATTN_W), seq_spec(KV_W), seq_spec(2 * KV_W), seq_spec(POOL_W),
                   seq_spec(CONV_W)),
        out_shape=out_shape,
        scratch_shapes=[pltpu.VMEM((seq + 2 * POOL_PAD, POOL_W), F32),
                        pltpu.VMEM((seq + 2 * CONV_PAD, CONV_W), F32),
                        pltpu.VMEM((SUBLANES - 1, seq + 2 * CONV_PAD, LANES), F32),
                        pltpu.VMEM((seq, CONV_W), F32)],
        compiler_params=pltpu.CompilerParams(dimension_semantics=("arbitrary",),
                                             vmem_limit_bytes=VMEM_LIMIT),
        name="front",
    )(x, *consts)


def _attn_kernel(q_ref, k_ref, v_ref, o_ref):
    group = N_HEADS // N_KV_HEADS

    def scores(head):
        kv = head // group
        q = q_ref[0, :, head * HEAD_DIM:(head + 1) * HEAD_DIM]
        k = k_ref[0, :, kv * HEAD_DIM:(kv + 1) * HEAD_DIM]
        return lax.dot_general(q, k, (((1,), (1,)), ((), ())), preferred_element_type=F32)

    s_next = scores(0)
    for head in range(N_HEADS):
        s = s_next
        if head + 1 < N_HEADS:
            s_next = scores(head + 1)
        kv = head // group
        v_ones = v_ref[0, :, 2 * kv * HEAD_DIM:2 * (kv + 1) * HEAD_DIM]
        m = jnp.max(s, axis=-1, keepdims=True)
        p = jnp.exp2(s - m).astype(BF16)
        o = jnp.dot(p, v_ones, preferred_element_type=F32)
        o = o / pltpu.roll(o, HEAD_DIM, 1)
        o_ref[0, :, head * HEAD_DIM:(head + 1) * HEAD_DIM] = o[:, :HEAD_DIM].astype(BF16)


def _attn_call(q, k, v_ones):
    batch, seq, _ = q.shape
    tq = min(ATTN_TQ, seq)
    return pl.pallas_call(
        _attn_kernel,
        grid=(batch, seq // tq),
        in_specs=[pl.BlockSpec((1, tq, ATTN_W), lambda b, i: (b, i, 0)),
                  pl.BlockSpec((1, seq, KV_W), lambda b, i: (b, 0, 0)),
                  pl.BlockSpec((1, seq, 2 * KV_W), lambda b, i: (b, 0, 0))],
        out_specs=pl.BlockSpec((1, tq, ATTN_W), lambda b, i: (b, i, 0)),
        out_shape=jax.ShapeDtypeStruct((batch, seq, ATTN_W), BF16),
        compiler_params=pltpu.CompilerParams(dimension_semantics=("arbitrary", "arbitrary"),
                                             vmem_limit_bytes=VMEM_LIMIT),
        name="attention",
    )(q, k, v_ones)


def _merge_kernel(x_ref, a_ref, p_ref, c_ref, wg_ref, bg_ref, wa_ref, wp_ref, wc_ref, wo_ref,
                  g_ref, b_ref, o_ref):
    sub = min(ROW_GROUP, x_ref.shape[0])
    for r in range(x_ref.shape[0] // sub):
        rows = slice(r * sub, (r + 1) * sub)
        x = x_ref[rows, :]
        xb = x.astype(BF16)
        merged = None
        for br, (in_ref, w_ref) in enumerate(((a_ref, wa_ref), (p_ref, wp_ref), (c_ref, wc_ref))):
            cols = slice(br * D_MODEL, (br + 1) * D_MODEL)
            gate = _sigmoid(jnp.dot(xb, wg_ref[:, cols], preferred_element_type=F32)
                            + bg_ref[:, cols])
            term = gate * jnp.dot(in_ref[rows, :], w_ref[...], preferred_element_type=F32)
            merged = term if merged is None else merged + term
        mix = jnp.dot(merged.astype(BF16), wo_ref[...], preferred_element_type=F32)
        o_ref[rows, :] = _layer_norm(ALPHA * x + mix, g_ref[...], b_ref[...])


def _merge_call(x, attn, pool, conv, mw):
    n = x.shape[0]
    tm = min(TOKEN_TILE, n)
    row_spec = lambda w: pl.BlockSpec((tm, w), lambda i: (i, 0))
    consts = (mw["w_gate"], mw["b_gate"], mw["w_attn_out"], mw["w_pool_out"], mw["w_conv_out"],
              mw["w_out"], mw["ln1_g"], mw["ln1_b"])
    return pl.pallas_call(
        _merge_kernel,
        grid=(n // tm,),
        in_specs=[row_spec(D_MODEL), row_spec(ATTN_W), row_spec(POOL_W), row_spec(CONV_W)]
        + [_const_spec(c.shape) for c in consts],
        out_specs=row_spec(D_MODEL),
        out_shape=jax.ShapeDtypeStruct((n, D_MODEL), F32),
        compiler_params=pltpu.CompilerParams(dimension_semantics=("arbitrary",),
                                             vmem_limit_bytes=VMEM_LIMIT),
        name="merge",
    )(x, attn, pool, conv, *consts)


def _swiglu_partial(xb, wg, wu, wd):
    hg = jnp.dot(xb, wg, preferred_element_type=F32)
    hu = jnp.dot(xb, wu, preferred_element_type=F32)
    h = (hg * _sigmoid(hg) * hu).astype(BF16)
    return jnp.dot(h, wd, preferred_element_type=F32)


def _ffn_kernel(x_ref, wg_ref, wu_ref, wd_ref, g_ref, b_ref, o_ref):
    half = D_FF // 2
    sub = min(ROW_GROUP, x_ref.shape[0])
    for r in range(x_ref.shape[0] // sub):
        rows = slice(r * sub, (r + 1) * sub)
        x = x_ref[rows, :]
        xb = x.astype(BF16)
        f = None
        for c in range(2):
            cols = slice(c * half, (c + 1) * half)
            part = _swiglu_partial(xb, wg_ref[:, cols], wu_ref[:, cols], wd_ref[cols, :])
            f = part if f is None else f + part
        o_ref[rows, :] = _layer_norm(ALPHA * x + f, g_ref[...], b_ref[...])


def _ffn_call(x, dw):
    n = x.shape[0]
    tm = min(TOKEN_TILE, n)
    row_spec = pl.BlockSpec((tm, D_MODEL), lambda i: (i, 0))
    consts = (dw["w_gate"], dw["w_up"], dw["w_down"], dw["ln2_g"], dw["ln2_b"])
    return pl.pallas_call(
        _ffn_kernel,
        grid=(n // tm,),
        in_specs=[row_spec] + [_const_spec(c.shape) for c in consts],
        out_specs=row_spec,
        out_shape=jax.ShapeDtypeStruct((n, D_MODEL), F32),
        compiler_params=pltpu.CompilerParams(dimension_semantics=("arbitrary",),
                                             vmem_limit_bytes=VMEM_LIMIT),
        name="ffn",
    )(x, *consts)


def _route(logits, lane):
    m1 = jnp.max(logits, axis=-1, keepdims=True)
    i1 = jnp.min(jnp.where(logits == m1, lane, LANES), axis=-1, keepdims=True)
    rest = jnp.where(lane == i1, NEG_BIG, logits)
    m2 = jnp.max(rest, axis=-1, keepdims=True)
    i2 = jnp.min(jnp.where(rest == m2, lane, LANES), axis=-1, keepdims=True)
    e2 = jnp.exp(m2 - m1)
    w1 = 1.0 / (1.0 + e2)
    w2 = e2 / (1.0 + e2)
    mask = jnp.where(lane == i1, 1.0, jnp.where(lane == i2, 1.0, 0.0))
    comb = jnp.where(lane == i1, w1, jnp.where(lane == i2, w2, 0.0))
    return mask, comb


def _moe_kernel(x_ref, wr_ref, br_ref, tri_ref, wg_ref, wu_ref, wd_ref, g_ref, b_ref, o_ref,
                comb_ref, rank_ref, rankt_ref, cnt_ref, offs_ref, xy_ref):
    tile = x_ref.shape[0]
    n_sub = tile // MOE_SUB
    e = pl.program_id(1)
    lane = lax.broadcasted_iota(jnp.int32, (MOE_SUB, LANES), 1)
    slot_rows = lax.broadcasted_iota(jnp.int32, (MOE_SLOTS, MOE_SUB), 0).astype(F32)
    slot_cols = lax.broadcasted_iota(jnp.int32, (MOE_SUB, MOE_SUB), 1).astype(F32)

    @pl.when(jnp.logical_and(pl.program_id(0) == 0, e == 0))
    def _():
        xy_ref[...] = jnp.zeros_like(xy_ref)

    @pl.when(e == 0)
    def _():
        for sb in range(n_sub):
            rows = slice(sb * MOE_SUB, (sb + 1) * MOE_SUB)
            xb = x_ref[rows, :].astype(BF16)
            comb_ref[rows, :] = jnp.dot(xb, wr_ref[...], preferred_element_type=F32) + br_ref[...]
        part = min(MOE_ROUTE_ROWS, tile)
        part_lane = lax.broadcasted_iota(jnp.int32, (part, LANES), 1)
        for c in range(tile // part):
            rows = slice(c * part, (c + 1) * part)
            mask, comb = _route(comb_ref[rows, :], part_lane)
            comb_ref[rows, :] = comb
            rank_ref[rows, :] = mask
        for sb in range(n_sub):
            rows = slice(sb * MOE_SUB, (sb + 1) * MOE_SUB)
            mask = rank_ref[rows, :]
            rank = jnp.dot(tri_ref[...], mask.astype(BF16), preferred_element_type=F32)
            rank = jnp.where(mask > 0.0, rank, -1.0)
            rank_ref[rows, :] = rank
            rankt_ref[:, rows] = rank.T
            counts = jnp.sum(mask, axis=0, keepdims=True).astype(jnp.int32)
            for ee in range(N_EXPERTS):
                cnt_ref[sb * N_EXPERTS + ee] = counts[0, ee]
        o_ref[...] = jnp.zeros_like(o_ref)

    def gather(sb, first_slot):
        rows = slice(sb * MOE_SUB, (sb + 1) * MOE_SUB)
        slot_of_token = rankt_ref[pl.ds(e, 1), rows] - first_slot
        take = jnp.where(slot_rows == slot_of_token, 1.0, 0.0).astype(BF16)
        return jnp.dot(take, x_ref[rows, :].astype(BF16), preferred_element_type=F32)

    off = 0
    for sb in range(n_sub):
        xy_ref[pl.ds(pl.multiple_of(off, MOE_ALIGN), MOE_SLOTS), :] = gather(sb, 0.0).astype(BF16)
        offs_ref[sb] = off
        count = cnt_ref[sb * N_EXPERTS + e]
        off = off + ((count + (MOE_ALIGN - 1)) // MOE_ALIGN) * MOE_ALIGN
    n_rows = off

    for sb in range(n_sub):
        count = cnt_ref[sb * N_EXPERTS + e]

        @pl.when(count > MOE_SLOTS)
        def _():
            start = pl.multiple_of(offs_ref[sb] + MOE_SLOTS, MOE_ALIGN)
            row = lax.broadcasted_iota(jnp.int32, (MOE_SLOTS, D_MODEL), 0)
            rest = gather(sb, float(MOE_SLOTS))
            keep = xy_ref[pl.ds(start, MOE_SLOTS), :].astype(F32)
            merged = jnp.where(row < count - MOE_SLOTS, rest, keep)
            xy_ref[pl.ds(start, MOE_SLOTS), :] = merged.astype(BF16)

    def expert_rows(start, size):
        xc = xy_ref[pl.ds(start, size), :]
        y = _swiglu_partial(xc, wg_ref[0], wu_ref[0], wd_ref[0])
        xy_ref[pl.ds(start, size), :] = y.astype(BF16)

    n_full = n_rows // MOE_CHUNK

    def full_body(i, carry):
        expert_rows(pl.multiple_of(i * MOE_CHUNK, MOE_CHUNK), MOE_CHUNK)
        return carry

    lax.fori_loop(0, n_full, full_body, 0)
    n_tail = (n_rows - n_full * MOE_CHUNK + (MOE_TAIL - 1)) // MOE_TAIL

    def tail_body(j, carry):
        expert_rows(pl.multiple_of(n_full * MOE_CHUNK + j * MOE_TAIL, MOE_TAIL), MOE_TAIL)
        return carry

    lax.fori_loop(0, n_tail, tail_body, 0)

    for sb in range(n_sub):
        rows = slice(sb * MOE_SUB, (sb + 1) * MOE_SUB)
        here = lane == e
        slot = jnp.sum(jnp.where(here, rank_ref[rows, :], 0.0), axis=-1, keepdims=True)
        weight = jnp.sum(jnp.where(here, comb_ref[rows, :], 0.0), axis=-1, keepdims=True)
        put = jnp.where(slot_cols == slot, 1.0, 0.0).astype(BF16)
        yb = xy_ref[pl.ds(pl.multiple_of(offs_ref[sb], MOE_ALIGN), MOE_SUB), :]
        o_ref[rows, :] += weight * jnp.dot(put, yb, preferred_element_type=F32)

    @pl.when(e == N_EXPERTS - 1)
    def _():
        o_ref[...] = _layer_norm(ALPHA * x_ref[...] + o_ref[...], g_ref[...], b_ref[...])


def _moe_call(x, ew):
    n = x.shape[0]
    tile = min(MOE_TILE, n)
    n_sub = tile // MOE_SUB
    row_spec = lambda **kw: pl.BlockSpec((tile, D_MODEL), lambda i, e: (i, 0), **kw)
    up_spec = pl.BlockSpec((1, D_MODEL, D_FF_EXPERT), lambda i, e: (e, 0, 0))
    down_spec = pl.BlockSpec((1, D_FF_EXPERT, D_MODEL), lambda i, e: (e, 0, 0))
    consts = (ew["w_router"], ew["b_router"], ew["tri"])
    small = (ew["ln2_g"], ew["ln2_b"])
    return pl.pallas_call(
        _moe_kernel,
        grid=(n // tile, N_EXPERTS),
        in_specs=[row_spec(pipeline_mode=pl.Buffered(1))] + [_const_spec(c.shape) for c in consts]
        + [up_spec, up_spec, down_spec] + [_const_spec(c.shape) for c in small],
        out_specs=row_spec(),
        out_shape=jax.ShapeDtypeStruct((n, D_MODEL), F32),
        scratch_shapes=[pltpu.VMEM((tile, LANES), F32), pltpu.VMEM((tile, LANES), F32),
                        pltpu.VMEM((LANES, tile), F32),
                        pltpu.SMEM((n_sub * N_EXPERTS,), jnp.int32),
                        pltpu.SMEM((n_sub,), jnp.int32),
                        pltpu.VMEM((tile + 2 * MOE_SUB, D_MODEL), BF16)],
        compiler_params=pltpu.CompilerParams(dimension_semantics=("arbitrary", "arbitrary"),
                                             vmem_limit_bytes=VMEM_LIMIT_HIGH),
        name="experts",
    )(x, *consts, ew["w_gate"], ew["w_up"], ew["w_down"], *small)


def _rope_tables(seq):
    rows = seq // GRID_W
    row = jnp.repeat(jnp.arange(rows), GRID_W).astype(F32)
    col = jnp.tile(jnp.arange(GRID_W), rows).astype(F32)
    n_freq = HEAD_DIM // 4
    inv = ROPE_THETA ** (-jnp.arange(n_freq, dtype=F32) / n_freq)
    ang = jnp.concatenate([row[:, None] * inv] * 2 + [col[:, None] * inv] * 2, axis=1)
    ang = jnp.concatenate([ang] * (LANES // HEAD_DIM), axis=1)
    first = (jnp.arange(LANES) % (2 * n_freq)) < n_freq
    cos, sin = jnp.cos(ang), jnp.sin(ang)
    return cos, jnp.where(first, -sin, 0.0), jnp.where(first, 0.0, sin)


def _row(v):
    return v.reshape(1, -1).astype(F32)


def _prepare(p, seq):
    cos, sin_a, sin_b = _rope_tables(seq)
    head = jnp.arange(ATTN_W) // HEAD_DIM
    seg = (head[:, None] == head[None, :]).astype(BF16)
    layers = []
    for l in range(DEPTH):
        w_in = p["w_in"][l]
        front = dict(
            w_front=w_in[:, :FRONT_W].astype(BF16), cos=cos, sin_a=sin_a, sin_b=sin_b,
            q_gain=_row(jnp.tile(p["q_norm_g"][l], N_HEADS)),
            k_gain=_row(jnp.tile(p["k_norm_g"][l], N_KV_HEADS)),
            seg=seg,
            w_mix=jax.scipy.linalg.block_diag(*p["w_pool_mix"][l]).astype(BF16),
            pool_scale=_row(p["pool_scale"][l]), w_dw=p["w_dw"][l].astype(F32),
            b_dw=_row(p["b_dw"][l]), conv_ln_g=_row(p["conv_ln_g"][l]),
            conv_ln_b=_row(p["conv_ln_b"][l]))
        merge = dict(
            w_gate=w_in[:, FRONT_W:].astype(BF16), b_gate=_row(p["b_gate"][l]),
            w_attn_out=p["w_attn_out"][l].astype(BF16), w_pool_out=p["w_pool_out"][l].astype(BF16),
            w_conv_out=p["w_conv_out"][l].astype(BF16), w_out=p["w_out"][l].astype(BF16),
            ln1_g=_row(p["ln1_g"][l]), ln1_b=_row(p["ln1_b"][l]))
        j = l // 2
        if l % 2 == 0:
            mixer = dict(w_gate=p["w_ff_gate"][j].astype(BF16), w_up=p["w_ff_up"][j].astype(BF16),
                         w_down=p["w_ff_down"][j].astype(BF16))
        else:
            pad = LANES - N_EXPERTS
            sub = jnp.arange(MOE_SUB)
            mixer = dict(
                tri=(sub[None, :] < sub[:, None]).astype(BF16),
                w_router=jnp.pad(p["w_router"][j], ((0, 0), (0, pad))).astype(BF16),
                b_router=jnp.pad(_row(p["b_router"][j]), ((0, 0), (0, pad)),
                                 constant_values=NEG_BIG),
                w_gate=p["w_e_gate"][j].astype(BF16), w_up=p["w_e_up"][j].astype(BF16),
                w_down=p["w_e_down"][j].astype(BF16))
        mixer["ln2_g"] = _row(p["ln2_g"][l])
        mixer["ln2_b"] = _row(p["ln2_b"][l])
        layers.append((front, merge, mixer))
    return layers


def _trunk(x, layers):
    batch, seq, _ = x.shape
    for l, (front, merge, mixer) in enumerate(layers):
        q, k, v, pool, conv = _front_call(x, front)
        attn = _attn_call(q, k, v)
        flat = lambda a: a.reshape(batch * seq, a.shape[-1])
        x1 = _merge_call(flat(x), flat(attn), flat(pool), flat(conv), merge)
        x2 = _ffn_call(x1, mixer) if l % 2 == 0 else _moe_call(x1, mixer)
        x = x2.reshape(batch, seq, D_MODEL)
    return x


def kernel(x_prompt, x_sample, w_in, b_gate, q_norm_g, k_norm_g, w_attn_out, w_pool_mix, pool_scale,
           w_pool_out, w_dw, b_dw, conv_ln_g, conv_ln_b, w_conv_out, w_out, ln1_g, ln1_b,
           w_ff_gate, w_ff_up, w_ff_down, w_router, b_router, w_e_gate, w_e_up, w_e_down,
           ln2_g, ln2_b):
    params = dict(w_in=w_in, b_gate=b_gate, q_norm_g=q_norm_g, k_norm_g=k_norm_g,
                  w_attn_out=w_attn_out, w_pool_mix=w_pool_mix, pool_scale=pool_scale,
                  w_pool_out=w_pool_out, w_dw=w_dw, b_dw=b_dw, conv_ln_g=conv_ln_g,
                  conv_ln_b=conv_ln_b, w_conv_out=w_conv_out, w_out=w_out, ln1_g=ln1_g, ln1_b=ln1_b,
                  w_ff_gate=w_ff_gate, w_ff_up=w_ff_up, w_ff_down=w_ff_down, w_router=w_router,
                  b_router=b_router, w_e_gate=w_e_gate, w_e_up=w_e_up, w_e_down=w_e_down,
                  ln2_g=ln2_g, ln2_b=ln2_b)
    assert x_prompt.shape[1] == x_sample.shape[1]
    layers = _prepare(params, x_prompt.shape[1])
    return _trunk(x_prompt, layers), _trunk(x_sample, layers)
```

```python
import math

import jax
import jax.numpy as jnp
from jax import lax
from jax.experimental import pallas as pl
from jax.experimental.pallas import tpu as pltpu

D_MODEL = 1024
DEPTH = 2
GRID_W = 64
N_HEADS = 8
N_KV_HEADS = 2
HEAD_DIM = 64
ATTN_W = N_HEADS * HEAD_DIM
KV_W = N_KV_HEADS * HEAD_DIM
ROPE_THETA = 10000.0
POOL_WINDOWS = (2, 4, 8, 16)
POOL_GROUP = 64
POOL_W = POOL_GROUP * len(POOL_WINDOWS)
CONV_W = 256
CONV_K = 31
N_BRANCH = 3
FRONT_W = ATTN_W + 2 * KV_W + POOL_W + 2 * CONV_W
D_FF = 2816
N_EXPERTS = 8
D_FF_EXPERT = 1408
ALPHA = (2 * DEPTH) ** 0.25
LN_EPS = 1e-5
RMS_EPS = 1e-6

LANES = 128
SUBLANES = 8
POOL_PAD = 8
CONV_PAD = 16
FRONT_ROWS = 256
FRONT_UNROLL = 8
CONV_ROWS = 128
ATTN_TQ = 512
LOG2_E = 1.4426950408889634
TOKEN_TILE = 1024
ROW_GROUP = 256
FUSED_TILE = 1024
MOE_TILE = 2048
MOE_SUB = 256
MOE_ROUTE_ROWS = 1024
MOE_ALIGN = 16
MOE_SLOTS = MOE_SUB // 2
MOE_CHUNK = 256
MOE_TAIL = 128
NEG_BIG = -1e30
VMEM_LIMIT = 56 * 1024 * 1024
VMEM_LIMIT_HIGH = 60 * 1024 * 1024

F32 = jnp.float32
BF16 = jnp.bfloat16


def _sigmoid(x):
    return 1.0 / (1.0 + jnp.exp(-x))


def _layer_norm(z, g, b):
    mu = jnp.mean(z, axis=-1, keepdims=True)
    d = z - mu
    var = jnp.mean(d * d, axis=-1, keepdims=True)
    return d * lax.rsqrt(var + LN_EPS) * g + b


def _const_spec(shape):
    zeros = (0,) * len(shape)
    return pl.BlockSpec(shape, lambda *_: zeros, pipeline_mode=pl.Buffered(1))


def _segment_sumsq(x, seg):
    sq = x * x
    hi = sq.astype(BF16)
    lo = (sq - hi.astype(F32)).astype(BF16)
    return (jnp.dot(hi, seg, preferred_element_type=F32)
            + jnp.dot(lo, seg, preferred_element_type=F32))


def _norm_rope(x, seg, gain, cos, sin_a, sin_b):
    width = x.shape[-1]
    ms = _segment_sumsq(x, seg) * (1.0 / HEAD_DIM)
    xn = x * lax.rsqrt(ms + RMS_EPS) * gain
    quarter = HEAD_DIM // 4
    up = pltpu.roll(xn, width - quarter, 1)
    down = pltpu.roll(xn, quarter, 1)
    return xn * cos + up * sin_a + down * sin_b


def _front_kernel(x_ref, w_ref, cos_ref, sina_ref, sinb_ref, qg_ref, kg_ref, seg_ref, wmix_ref,
                  pscale_ref, wdw_ref, bdw_ref, lng_ref, lnb_ref,
                  q_ref, k_ref, v_ref, pool_ref, conv_ref, ubuf, hbuf, hs_ref, cacc_ref):
    seq = x_ref.shape[1]
    rows = min(FRONT_ROWS, seq)
    o1, o2, o3 = ATTN_W, ATTN_W + KV_W, ATTN_W + 2 * KV_W
    o4, o5 = o3 + POOL_W, o3 + POOL_W + CONV_W

    ubuf[0:POOL_PAD, :] = jnp.zeros((POOL_PAD, POOL_W), F32)
    ubuf[POOL_PAD + seq:, :] = jnp.zeros((POOL_PAD, POOL_W), F32)
    hbuf[0:CONV_PAD, :] = jnp.zeros((CONV_PAD, CONV_W), F32)
    hbuf[CONV_PAD + seq:, :] = jnp.zeros((CONV_PAD, CONV_W), F32)

    def proj_chunk(c, carry):
        r0 = pl.multiple_of(c * rows, rows)
        xb = x_ref[0, pl.ds(r0, rows), :].astype(BF16)
        proj = jnp.dot(xb, w_ref[...], preferred_element_type=F32)
        cos = cos_ref[pl.ds(r0, rows), :]
        sin_a = sina_ref[pl.ds(r0, rows), :]
        sin_b = sinb_ref[pl.ds(r0, rows), :]
        rep = ATTN_W // LANES
        q = _norm_rope(proj[:, :o1], seg_ref[...], qg_ref[...],
                       jnp.concatenate([cos] * rep, axis=1),
                       jnp.concatenate([sin_a] * rep, axis=1),
                       jnp.concatenate([sin_b] * rep, axis=1))
        q_ref[0, pl.ds(r0, rows), :] = (q * (HEAD_DIM ** -0.5 * LOG2_E)).astype(BF16)
        k = _norm_rope(proj[:, o1:o2], seg_ref[0:KV_W, 0:KV_W], kg_ref[...], cos, sin_a, sin_b)
        k_ref[0, pl.ds(r0, rows), :] = k.astype(BF16)
        v = proj[:, o2:o3]
        low = lax.broadcasted_iota(jnp.int32, v.shape, 1) < HEAD_DIM
        v_ones = [jnp.where(low, v, 1.0), jnp.where(low, pltpu.roll(v, HEAD_DIM, 1), 1.0)]
        v_ref[0, pl.ds(r0, rows), :] = jnp.concatenate(v_ones, axis=1).astype(BF16)
        ubuf[pl.ds(POOL_PAD + r0, rows), :] = proj[:, o3:o4]
        hbuf[pl.ds(CONV_PAD + r0, rows), :] = proj[:, o4:o5] * _sigmoid(proj[:, o5:])
        return carry

    n_chunks = seq // rows
    lax.fori_loop(0, n_chunks, proj_chunk, 0, unroll=math.gcd(n_chunks, FRONT_UNROLL))

    lane = lax.broadcasted_iota(jnp.int32, (rows, LANES), 1)
    first = lane < POOL_GROUP
    for c in range(seq // rows):
        r0 = c * rows
        t = r0 + lax.broadcasted_iota(jnp.int32, (rows, LANES), 0)
        halves = []
        for half in range(POOL_W // LANES):
            w_a, w_b = POOL_WINDOWS[2 * half], POOL_WINDOWS[2 * half + 1]
            lo_a, hi_a = w_a // 2, w_a - w_a // 2 - 1
            lo_b, hi_b = w_b // 2, w_b - w_b // 2 - 1
            cols = slice(half * LANES, (half + 1) * LANES)

            def shifted(off):
                return ubuf[POOL_PAD + r0 + off:POOL_PAD + r0 + off + rows, cols]

            centre = shifted(0)
            tot_a = centre
            for off in range(-lo_a, hi_a + 1):
                if off != 0:
                    tot_a = tot_a + shifted(off)
            tot_b = tot_a
            for off in range(-lo_b, hi_b + 1):
                if off < -lo_a or off > hi_a:
                    tot_b = tot_b + shifted(off)
            lo = jnp.where(first, lo_a, lo_b)
            hi = jnp.where(first, hi_a, hi_b)
            cnt = (jnp.minimum(t + hi + 1, seq) - jnp.maximum(t - lo, 0)).astype(F32)
            tot = jnp.where(first, tot_a, tot_b)
            halves.append(tot / cnt - centre)
        pooled = jnp.concatenate(halves, axis=1).astype(BF16)
        mixed = jnp.dot(pooled, wmix_ref[...], preferred_element_type=F32) * pscale_ref[...]
        pool_ref[0, r0:r0 + rows, :] = mixed.astype(BF16)

    crow = min(CONV_ROWS, seq)
    shift0 = CONV_PAD - CONV_K // 2
    tail_rows = 2 * CONV_PAD - SUBLANES
    n_half = CONV_W // LANES
    for half in range(n_half):
        cols = slice(half * LANES, (half + 1) * LANES)

        def shift_chunk(c, carry):
            j0 = pl.multiple_of(c * crow, crow)
            win = hbuf[pl.ds(j0, crow + SUBLANES), cols]
            for s in range(1, SUBLANES):
                hs_ref[s - 1, pl.ds(j0, crow), :] = win[s:s + crow, :]
            return carry

        lax.fori_loop(0, seq // crow, shift_chunk, 0)
        tail = hbuf[seq:seq + 2 * CONV_PAD, cols]
        for s in range(1, SUBLANES):
            hs_ref[s - 1, seq:seq + tail_rows, :] = tail[s:s + tail_rows, :]

        def tap_chunk(c, carry):
            r0 = pl.multiple_of(c * crow, crow)
            acc = jnp.broadcast_to(bdw_ref[:, cols], (crow, LANES))
            for kk in range(CONV_K):
                a, s = divmod(kk + shift0, SUBLANES)
                start = r0 + SUBLANES * a
                src = hbuf[pl.ds(start, crow), cols] if s == 0 else hs_ref[s - 1, pl.ds(start, crow), :]
                acc = acc + src * wdw_ref[kk:kk + 1, cols]
            cacc_ref[pl.ds(r0, crow), cols] = acc
            return carry

        lax.fori_loop(0, seq // crow, tap_chunk, 0)

    def norm_chunk(c, carry):
        r0 = pl.multiple_of(c * rows, rows)
        hn = _layer_norm(cacc_ref[pl.ds(r0, rows), :], lng_ref[...], lnb_ref[...])
        conv_ref[0, pl.ds(r0, rows), :] = (hn * _sigmoid(hn)).astype(BF16)
        return carry

    lax.fori_loop(0, n_chunks, norm_chunk, 0, unroll=math.gcd(n_chunks, FRONT_UNROLL))


def _front_call(x, fw):
    batch, seq, _ = x.shape
    seq_spec = lambda w: pl.BlockSpec((1, seq, w), lambda b: (b, 0, 0))
    out_shape = (
        jax.ShapeDtypeStruct((batch, seq, ATTN_W), BF16),
        jax.ShapeDtypeStruct((batch, seq, KV_W), BF16),
        jax.ShapeDtypeStruct((batch, seq, 2 * KV_W), BF16),
        jax.ShapeDtypeStruct((batch, seq, POOL_W), BF16),
        jax.ShapeDtypeStruct((batch, seq, CONV_W), BF16),
    )
    consts = (fw["w_front"], fw["cos"], fw["sin_a"], fw["sin_b"], fw["q_gain"], fw["k_gain"],
              fw["seg"], fw["w_mix"], fw["pool_scale"], fw["w_dw"], fw["b_dw"], fw["conv_ln_g"],
              fw["conv_ln_b"])
    return pl.pallas_call(
        _front_kernel,
        grid=(batch,),
        in_specs=[seq_spec(D_MODEL)] + [_const_spec(c.shape) for c in consts],
        out_specs=(seq_spec(ATTN_W), seq_spec(KV_W), seq_spec(2 * KV_W), seq_spec(POOL_W),
                   seq_spec(CONV_W)),
        out_shape=out_shape,
        scratch_shapes=[pltpu.VMEM((seq + 2 * POOL_PAD, POOL_W), F32),
                        pltpu.VMEM((seq + 2 * CONV_PAD, CONV_W), F32),
                        pltpu.VMEM((SUBLANES - 1, seq + 2 * CONV_PAD, LANES), F32),
                        pltpu.VMEM((seq, CONV_W), F32)],
        compiler_params=pltpu.CompilerParams(dimension_semantics=("arbitrary",),
                                             vmem_limit_bytes=VMEM_LIMIT),
        name="front",
    )(x, *consts)


def _attn_kernel(q_ref, k_ref, v_ref, o_ref):
    group = N_HEADS // N_KV_HEADS

    def scores(head):
        kv = head // group
        q = q_ref[0, :, head * HEAD_DIM:(head + 1) * HEAD_DIM]
        k = k_ref[0, :, kv * HEAD_DIM:(kv + 1) * HEAD_DIM]
        return lax.dot_general(q, k, (((1,), (1,)), ((), ())), preferred_element_type=F32)

    s_next = scores(0)
    for head in range(N_HEADS):
        s = s_next
        if head + 1 < N_HEADS:
            s_next = scores(head + 1)
        kv = head // group
        v_ones = v_ref[0, :, 2 * kv * HEAD_DIM:2 * (kv + 1) * HEAD_DIM]
        m = jnp.max(s, axis=-1, keepdims=True)
        p = jnp.exp2(s - m).astype(BF16)
        o = jnp.dot(p, v_ones, preferred_element_type=F32)
        o = o / pltpu.roll(o, HEAD_DIM, 1)
        o_ref[0, :, head * HEAD_DIM:(head + 1) * HEAD_DIM] = o[:, :HEAD_DIM].astype(BF16)


def _attn_call(q, k, v_ones):
    batch, seq, _ = q.shape
    tq = min(ATTN_TQ, seq)
    return pl.pallas_call(
        _attn_kernel,
        grid=(batch, seq // tq),
        in_specs=[pl.BlockSpec((1, tq, ATTN_W), lambda b, i: (b, i, 0)),
                  pl.BlockSpec((1, seq, KV_W), lambda b, i: (b, 0, 0)),
                  pl.BlockSpec((1, seq, 2 * KV_W), lambda b, i: (b, 0, 0))],
        out_specs=pl.BlockSpec((1, tq, ATTN_W), lambda b, i: (b, i, 0)),
        out_shape=jax.ShapeDtypeStruct((batch, seq, ATTN_W), BF16),
        compiler_params=pltpu.CompilerParams(dimension_semantics=("arbitrary", "arbitrary"),
                                             vmem_limit_bytes=VMEM_LIMIT),
        name="attention",
    )(q, k, v_ones)


def _merge_rows(rows, x_ref, a_ref, p_ref, c_ref, wg_ref, bg_ref, wa_ref, wp_ref, wc_ref, wo_ref,
                g_ref, b_ref):
    x = x_ref[rows, :]
    xb = x.astype(BF16)
    merged = None
    for br, (in_ref, w_ref) in enumerate(((a_ref, wa_ref), (p_ref, wp_ref), (c_ref, wc_ref))):
        cols = slice(br * D_MODEL, (br + 1) * D_MODEL)
        gate = _sigmoid(jnp.dot(xb, wg_ref[:, cols], preferred_element_type=F32) + bg_ref[:, cols])
        term = gate * jnp.dot(in_ref[rows, :], w_ref[...], preferred_element_type=F32)
        merged = term if merged is None else merged + term
    mix = jnp.dot(merged.astype(BF16), wo_ref[...], preferred_element_type=F32)
    return _layer_norm(ALPHA * x + mix, g_ref[...], b_ref[...])


def _ffn_rows(x, wg_ref, wu_ref, wd_ref, g_ref, b_ref):
    xb = x.astype(BF16)
    half = D_FF // 2
    f = None
    for c in range(2):
        cols = slice(c * half, (c + 1) * half)
        part = _swiglu_partial(xb, wg_ref[:, cols], wu_ref[:, cols], wd_ref[cols, :])
        f = part if f is None else f + part
    return _layer_norm(ALPHA * x + f, g_ref[...], b_ref[...])


def _merge_kernel(x_ref, *refs):
    o_ref = refs[-1]
    sub = min(ROW_GROUP, x_ref.shape[0])
    for r in range(x_ref.shape[0] // sub):
        rows = slice(r * sub, (r + 1) * sub)
        o_ref[rows, :] = _merge_rows(rows, x_ref, *refs[:-1])


def _merge_ffn_kernel(x_ref, *refs):
    merge_refs, ffn_refs, o_ref = refs[:11], refs[11:16], refs[16]
    sub = min(ROW_GROUP, x_ref.shape[0])
    for r in range(x_ref.shape[0] // sub):
        rows = slice(r * sub, (r + 1) * sub)
        o_ref[rows, :] = _ffn_rows(_merge_rows(rows, x_ref, *merge_refs), *ffn_refs)


def _merge_ffn_call(x, attn, pool, conv, mw, dw):
    n = x.shape[0]
    tm = min(FUSED_TILE, n)
    row_spec = lambda w: pl.BlockSpec((tm, w), lambda i: (i, 0))
    consts = (mw["w_gate"], mw["b_gate"], mw["w_attn_out"], mw["w_pool_out"], mw["w_conv_out"],
              mw["w_out"], mw["ln1_g"], mw["ln1_b"],
              dw["w_gate"], dw["w_up"], dw["w_down"], dw["ln2_g"], dw["ln2_b"])
    return pl.pallas_call(
        _merge_ffn_kernel,
        grid=(n // tm,),
        in_specs=[row_spec(D_MODEL), row_spec(ATTN_W), row_spec(POOL_W), row_spec(CONV_W)]
        + [_const_spec(c.shape) for c in consts],
        out_specs=row_spec(D_MODEL),
        out_shape=jax.ShapeDtypeStruct((n, D_MODEL), F32),
        compiler_params=pltpu.CompilerParams(dimension_semantics=("arbitrary",),
                                             vmem_limit_bytes=VMEM_LIMIT),
        name="merge_ffn",
    )(x, attn, pool, conv, *consts)


def _merge_call(x, attn, pool, conv, mw):
    n = x.shape[0]
    tm = min(TOKEN_TILE, n)
    row_spec = lambda w: pl.BlockSpec((tm, w), lambda i: (i, 0))
    consts = (mw["w_gate"], mw["b_gate"], mw["w_attn_out"], mw["w_pool_out"], mw["w_conv_out"],
              mw["w_out"], mw["ln1_g"], mw["ln1_b"])
    return pl.pallas_call(
        _merge_kernel,
        grid=(n // tm,),
        in_specs=[row_spec(D_MODEL), row_spec(ATTN_W), row_spec(POOL_W), row_spec(CONV_W)]
        + [_const_spec(c.shape) for c in consts],
        out_specs=row_spec(D_MODEL),
        out_shape=jax.ShapeDtypeStruct((n, D_MODEL), F32),
        compiler_params=pltpu.CompilerParams(dimension_semantics=("arbitrary",),
                                             vmem_limit_bytes=VMEM_LIMIT),
        name="merge",
    )(x, attn, pool, conv, *consts)


def _swiglu_partial(xb, wg, wu, wd):
    hg = jnp.dot(xb, wg, preferred_element_type=F32)
    hu = jnp.dot(xb, wu, preferred_element_type=F32)
    h = (hg * _sigmoid(hg) * hu).astype(BF16)
    return jnp.dot(h, wd, preferred_element_type=F32)


def _route(logits, lane):
    m1 = jnp.max(logits, axis=-1, keepdims=True)
    i1 = jnp.min(jnp.where(logits == m1, lane, LANES), axis=-1, keepdims=True)
    rest = jnp.where(lane == i1, NEG_BIG, logits)
    m2 = jnp.max(rest, axis=-1, keepdims=True)
    i2 = jnp.min(jnp.where(rest == m2, lane, LANES), axis=-1, keepdims=True)
    e2 = jnp.exp(m2 - m1)
    w1 = 1.0 / (1.0 + e2)
    w2 = e2 / (1.0 + e2)
    mask = jnp.where(lane == i1, 1.0, jnp.where(lane == i2, 1.0, 0.0))
    comb = jnp.where(lane == i1, w1, jnp.where(lane == i2, w2, 0.0))
    return mask, comb


def _moe_kernel(x_ref, wr_ref, br_ref, tri_ref, wg_ref, wu_ref, wd_ref, g_ref, b_ref, o_ref,
                comb_ref, rank_ref, rankt_ref, cnt_ref, offs_ref, xy_ref):
    tile = x_ref.shape[0]
    n_sub = tile // MOE_SUB
    e = pl.program_id(1)
    lane = lax.broadcasted_iota(jnp.int32, (MOE_SUB, LANES), 1)
    slot_rows = lax.broadcasted_iota(jnp.int32, (MOE_SLOTS, MOE_SUB), 0).astype(F32)
    slot_cols = lax.broadcasted_iota(jnp.int32, (MOE_SUB, MOE_SUB), 1).astype(F32)

    @pl.when(jnp.logical_and(pl.program_id(0) == 0, e == 0))
    def _():
        xy_ref[...] = jnp.zeros_like(xy_ref)

    @pl.when(e == 0)
    def _():
        for sb in range(n_sub):
            rows = slice(sb * MOE_SUB, (sb + 1) * MOE_SUB)
            xb = x_ref[rows, :].astype(BF16)
            comb_ref[rows, :] = jnp.dot(xb, wr_ref[...], preferred_element_type=F32) + br_ref[...]
        part = min(MOE_ROUTE_ROWS, tile)
        part_lane = lax.broadcasted_iota(jnp.int32, (part, LANES), 1)
        for c in range(tile // part):
            rows = slice(c * part, (c + 1) * part)
            mask, comb = _route(comb_ref[rows, :], part_lane)
            comb_ref[rows, :] = comb
            rank_ref[rows, :] = mask
        for sb in range(n_sub):
            rows = slice(sb * MOE_SUB, (sb + 1) * MOE_SUB)
            mask = rank_ref[rows, :]
            rank = jnp.dot(tri_ref[...], mask.astype(BF16), preferred_element_type=F32)
            rank = jnp.where(mask > 0.0, rank, -1.0)
            rank_ref[rows, :] = rank
            rankt_ref[:, rows] = rank.T
            counts = jnp.sum(mask, axis=0, keepdims=True).astype(jnp.int32)
            for ee in range(N_EXPERTS):
                cnt_ref[sb * N_EXPERTS + ee] = counts[0, ee]
        o_ref[...] = jnp.zeros_like(o_ref)

    def gather(sb, first_slot):
        rows = slice(sb * MOE_SUB, (sb + 1) * MOE_SUB)
        slot_of_token = rankt_ref[pl.ds(e, 1), rows] - first_slot
        take = jnp.where(slot_rows == slot_of_token, 1.0, 0.0).astype(BF16)
        return jnp.dot(take, x_ref[rows, :].astype(BF16), preferred_element_type=F32)

    off = 0
    for sb in range(n_sub):
        xy_ref[pl.ds(pl.multiple_of(off, MOE_ALIGN), MOE_SLOTS), :] = gather(sb, 0.0).astype(BF16)
        offs_ref[sb] = off
        count = cnt_ref[sb * N_EXPERTS + e]
        off = off + ((count + (MOE_ALIGN - 1)) // MOE_ALIGN) * MOE_ALIGN
    n_rows = off

    for sb in range(n_sub):
        count = cnt_ref[sb * N_EXPERTS + e]

        @pl.when(count > MOE_SLOTS)
        def _():
            start = pl.multiple_of(offs_ref[sb] + MOE_SLOTS, MOE_ALIGN)
            row = lax.broadcasted_iota(jnp.int32, (MOE_SLOTS, D_MODEL), 0)
            rest = gather(sb, float(MOE_SLOTS))
            keep = xy_ref[pl.ds(start, MOE_SLOTS), :].astype(F32)
            merged = jnp.where(row < count - MOE_SLOTS, rest, keep)
            xy_ref[pl.ds(start, MOE_SLOTS), :] = merged.astype(BF16)

    def expert_rows(start, size):
        xc = xy_ref[pl.ds(start, size), :]
        y = _swiglu_partial(xc, wg_ref[0], wu_ref[0], wd_ref[0])
        xy_ref[pl.ds(start, size), :] = y.astype(BF16)

    n_full = n_rows // MOE_CHUNK

    def full_body(i, carry):
        expert_rows(pl.multiple_of(i * MOE_CHUNK, MOE_CHUNK), MOE_CHUNK)
        return carry

    lax.fori_loop(0, n_full, full_body, 0)
    n_tail = (n_rows - n_full * MOE_CHUNK + (MOE_TAIL - 1)) // MOE_TAIL

    def tail_body(j, carry):
        expert_rows(pl.multiple_of(n_full * MOE_CHUNK + j * MOE_TAIL, MOE_TAIL), MOE_TAIL)
        return carry

    lax.fori_loop(0, n_tail, tail_body, 0)

    for sb in range(n_sub):
        rows = slice(sb * MOE_SUB, (sb + 1) * MOE_SUB)
        here = lane == e
        slot = jnp.sum(jnp.where(here, rank_ref[rows, :], 0.0), axis=-1, keepdims=True)
        weight = jnp.sum(jnp.where(here, comb_ref[rows, :], 0.0), axis=-1, keepdims=True)
        put = jnp.where(slot_cols == slot, 1.0, 0.0).astype(BF16)
        yb = xy_ref[pl.ds(pl.multiple_of(offs_ref[sb], MOE_ALIGN), MOE_SUB), :]
        o_ref[rows, :] += weight * jnp.dot(put, yb, preferred_element_type=F32)

    @pl.when(e == N_EXPERTS - 1)
    def _():
        o_ref[...] = _layer_norm(ALPHA * x_ref[...] + o_ref[...], g_ref[...], b_ref[...])


def _moe_call(x, ew):
    n = x.shape[0]
    tile = min(MOE_TILE, n)
    n_sub = tile // MOE_SUB
    row_spec = lambda **kw: pl.BlockSpec((tile, D_MODEL), lambda i, e: (i, 0), **kw)
    up_spec = pl.BlockSpec((1, D_MODEL, D_FF_EXPERT), lambda i, e: (e, 0, 0))
    down_spec = pl.BlockSpec((1, D_FF_EXPERT, D_MODEL), lambda i, e: (e, 0, 0))
    consts = (ew["w_router"], ew["b_router"], ew["tri"])
    small = (ew["ln2_g"], ew["ln2_b"])
    return pl.pallas_call(
        _moe_kernel,
        grid=(n // tile, N_EXPERTS),
        in_specs=[row_spec(pipeline_mode=pl.Buffered(1))] + [_const_spec(c.shape) for c in consts]
        + [up_spec, up_spec, down_spec] + [_const_spec(c.shape) for c in small],
        out_specs=row_spec(),
        out_shape=jax.ShapeDtypeStruct((n, D_MODEL), F32),
        scratch_shapes=[pltpu.VMEM((tile, LANES), F32), pltpu.VMEM((tile, LANES), F32),
                        pltpu.VMEM((LANES, tile), F32),
                        pltpu.SMEM((n_sub * N_EXPERTS,), jnp.int32),
                        pltpu.SMEM((n_sub,), jnp.int32),
                        pltpu.VMEM((tile + 2 * MOE_SUB, D_MODEL), BF16)],
        compiler_params=pltpu.CompilerParams(dimension_semantics=("arbitrary", "arbitrary"),
                                             vmem_limit_bytes=VMEM_LIMIT_HIGH),
        name="experts",
    )(x, *consts, ew["w_gate"], ew["w_up"], ew["w_down"], *small)


def _rope_tables(seq):
    rows = seq // GRID_W
    row = jnp.repeat(jnp.arange(rows), GRID_W).astype(F32)
    col = jnp.tile(jnp.arange(GRID_W), rows).astype(F32)
    n_freq = HEAD_DIM // 4
    inv = ROPE_THETA ** (-jnp.arange(n_freq, dtype=F32) / n_freq)
    ang = jnp.concatenate([row[:, None] * inv] * 2 + [col[:, None] * inv] * 2, axis=1)
    ang = jnp.concatenate([ang] * (LANES // HEAD_DIM), axis=1)
    first = (jnp.arange(LANES) % (2 * n_freq)) < n_freq
    cos, sin = jnp.cos(ang), jnp.sin(ang)
    return cos, jnp.where(first, -sin, 0.0), jnp.where(first, 0.0, sin)


def _row(v):
    return v.reshape(1, -1).astype(F32)


def _prepare(p, seq):
    cos, sin_a, sin_b = _rope_tables(seq)
    head = jnp.arange(ATTN_W) // HEAD_DIM
    seg = (head[:, None] == head[None, :]).astype(BF16)
    layers = []
    for l in range(DEPTH):
        w_in = p["w_in"][l]
        front = dict(
            w_front=w_in[:, :FRONT_W].astype(BF16), cos=cos, sin_a=sin_a, sin_b=sin_b,
            q_gain=_row(jnp.tile(p["q_norm_g"][l], N_HEADS)),
            k_gain=_row(jnp.tile(p["k_norm_g"][l], N_KV_HEADS)),
            seg=seg,
            w_mix=jax.scipy.linalg.block_diag(*p["w_pool_mix"][l]).astype(BF16),
            pool_scale=_row(p["pool_scale"][l]), w_dw=p["w_dw"][l].astype(F32),
            b_dw=_row(p["b_dw"][l]), conv_ln_g=_row(p["conv_ln_g"][l]),
            conv_ln_b=_row(p["conv_ln_b"][l]))
        merge = dict(
            w_gate=w_in[:, FRONT_W:].astype(BF16), b_gate=_row(p["b_gate"][l]),
            w_attn_out=p["w_attn_out"][l].astype(BF16), w_pool_out=p["w_pool_out"][l].astype(BF16),
            w_conv_out=p["w_conv_out"][l].astype(BF16), w_out=p["w_out"][l].astype(BF16),
            ln1_g=_row(p["ln1_g"][l]), ln1_b=_row(p["ln1_b"][l]))
        j = l // 2
        if l % 2 == 0:
            mixer = dict(w_gate=p["w_ff_gate"][j].astype(BF16), w_up=p["w_ff_up"][j].astype(BF16),
                         w_down=p["w_ff_down"][j].astype(BF16))
        else:
            pad = LANES - N_EXPERTS
            sub = jnp.arange(MOE_SUB)
            mixer = dict(
                tri=(sub[None, :] < sub[:, None]).astype(BF16),
                w_router=jnp.pad(p["w_router"][j], ((0, 0), (0, pad))).astype(BF16),
                b_router=jnp.pad(_row(p["b_router"][j]), ((0, 0), (0, pad)),
                                 constant_values=NEG_BIG),
                w_gate=p["w_e_gate"][j].astype(BF16), w_up=p["w_e_up"][j].astype(BF16),
                w_down=p["w_e_down"][j].astype(BF16))
        mixer["ln2_g"] = _row(p["ln2_g"][l])
        mixer["ln2_b"] = _row(p["ln2_b"][l])
        layers.append((front, merge, mixer))
    return layers


def _trunk(x, layers):
    batch, seq, _ = x.shape
    for l, (front, merge, mixer) in enumerate(layers):
        q, k, v, pool, conv = _front_call(x, front)
        attn = _attn_call(q, k, v)
        flat = lambda a: a.reshape(batch * seq, a.shape[-1])
        branches = (flat(x), flat(attn), flat(pool), flat(conv))
        if l % 2 == 0:
            x2 = _merge_ffn_call(*branches, merge, mixer)
        else:
            x2 = _moe_call(_merge_call(*branches, merge), mixer)
        x = x2.reshape(batch, seq, D_MODEL)
    return x


def kernel(x_prompt, x_sample, w_in, b_gate, q_norm_g, k_norm_g, w_attn_out, w_pool_mix, pool_scale,
           w_pool_out, w_dw, b_dw, conv_ln_g, conv_ln_b, w_conv_out, w_out, ln1_g, ln1_b,
           w_ff_gate, w_ff_up, w_ff_down, w_router, b_router, w_e_gate, w_e_up, w_e_down,
           ln2_g, ln2_b):
    params = dict(w_in=w_in, b_gate=b_gate, q_norm_g=q_norm_g, k_norm_g=k_norm_g,
                  w_attn_out=w_attn_out, w_pool_mix=w_pool_mix, pool_scale=pool_scale,
                  w_pool_out=w_pool_out, w_dw=w_dw, b_dw=b_dw, conv_ln_g=conv_ln_g,
                  conv_ln_b=conv_ln_b, w_conv_out=w_conv_out, w_out=w_out, ln1_g=ln1_g, ln1_b=ln1_b,
                  w_ff_gate=w_ff_gate, w_ff_up=w_ff_up, w_ff_down=w_ff_down, w_router=w_router,
                  b_router=b_router, w_e_gate=w_e_gate, w_e_up=w_e_up, w_e_down=w_e_down,
                  ln2_g=ln2_g, ln2_b=ln2_b)
    assert x_prompt.shape[1] == x_sample.shape[1]
    layers = _prepare(params, x_prompt.shape[1])
    return _trunk(x_prompt, layers), _trunk(x_sample, layers)
```
